```python
import math
import jax
import jax.numpy as jnp
from jax import lax
import numpy as np

D_MODEL = 2048
BATCH = 1
SEQ = 16384
DEPTH = 1
DEC_BATCH = 32
DEC_SEQ = 8
PAST_LEN = 16384
PAGE_SIZE = 128

SSM_WIDTH = D_MODEL // 2
SSM_GROUP = 16
SSM_GROUPS = SSM_WIDTH // SSM_GROUP
SSM_STATE = 64
SB_HEAD_DIM = 128
SB_HEADS = D_MODEL // 256
SB_WIDTH = SB_HEADS * SB_HEAD_DIM
Q_BLOCK = 128
SB_BIAS_NEAR = -4.0
SB_BIAS_FAR = -10.0
MEM_TOKENS = 256
MEM_HEADS = 4
MEM_HEAD_DIM = D_MODEL // 8
MEM_WIDTH = MEM_HEADS * MEM_HEAD_DIM
N_BRANCHES = 3
IN_SPLITS = (SSM_WIDTH, SSM_WIDTH + SB_WIDTH, SSM_WIDTH + 2 * SB_WIDTH,
             SSM_WIDTH + 3 * SB_WIDTH, SSM_WIDTH + 3 * SB_WIDTH + MEM_WIDTH)
IN_WIDTH = SSM_WIDTH + 3 * SB_WIDTH + MEM_WIDTH + N_BRANCHES * D_MODEL
N_GROUPS = 4
EXPERTS_PER_GROUP = 8
N_EXPERTS = N_GROUPS * EXPERTS_PER_GROUP
EXPERT_FF = D_MODEL // 8
TOP_K = 2
RMS_EPS = 1e-6
F32 = jnp.float32

kernel_name = 'hybrid_s5_stickbreak_memx_hmoe_step'


def rmsnorm(x, g):
    xf = x.astype(F32)
    y = xf * lax.rsqrt(jnp.mean(xf * xf, axis=-1, keepdims=True) + RMS_EPS)
    return (y * g.astype(F32)).astype(x.dtype)


def s5_discretize(lam_re, lam_im, log_step):
    lre = lam_re.astype(F32)
    lim = lam_im.astype(F32)
    dt = jnp.exp(log_step.astype(F32))[:, None]
    mag = jnp.exp(lre * dt)
    phase = lim * dt
    a_re = mag * jnp.cos(phase)
    a_im = mag * jnp.sin(phase)
    num_re = a_re - 1.0
    den = lre * lre + lim * lim
    k_re = (num_re * lre + a_im * lim) / den
    k_im = (a_im * lre - num_re * lim) / den
    return a_re, a_im, k_re, k_im


def linear_recurrence_combine(e1, e2):
    a1r, a1i, b1r, b1i = e1
    a2r, a2i, b2r, b2i = e2
    return (a2r * a1r - a2i * a1i,
            a2r * a1i + a2i * a1r,
            a2r * b1r - a2i * b1i + b2r,
            a2r * b1i + a2i * b1r + b2i)


def s5_branch(u, h0_re, h0_im, lam_re, lam_im, log_step, b_re, b_im, c_re, c_im, d_skip, w_glu, b_glu):
    bsz, t, _ = u.shape
    uf = u.astype(F32).reshape(bsz, t, SSM_GROUPS, SSM_GROUP)
    a_re, a_im, k_re, k_im = s5_discretize(lam_re, lam_im, log_step)
    bu_re = jnp.einsum('btgc,gpc->btgp', uf, b_re.astype(F32))
    bu_im = jnp.einsum('btgc,gpc->btgp', uf, b_im.astype(F32))
    x_re = k_re * bu_re - k_im * bu_im
    x_im = k_re * bu_im + k_im * bu_re
    h0r = h0_re.astype(F32)
    h0i = h0_im.astype(F32)
    x_re = x_re.at[:, 0].add(a_re * h0r - a_im * h0i)
    x_im = x_im.at[:, 0].add(a_re * h0i + a_im * h0r)
    elems = (jnp.broadcast_to(a_re, x_re.shape), jnp.broadcast_to(a_im, x_re.shape), x_re, x_im)
    _, _, h_re, h_im = lax.associative_scan(linear_recurrence_combine, elems, axis=1)
    y = (jnp.einsum('btgp,gcp->btgc', h_re, c_re.astype(F32))
         - jnp.einsum('btgp,gcp->btgc', h_im, c_im.astype(F32))
         + d_skip.astype(F32) * uf)
    z = jax.nn.gelu(y.reshape(bsz, t, SSM_WIDTH))
    out = z * jax.nn.sigmoid(z @ w_glu.astype(F32) + b_glu.astype(F32))
    return out.astype(u.dtype), h_re[:, -1], h_im[:, -1]


def stick_breaking_weights(z, mask):
    log_keep = jnp.where(mask, jax.nn.log_sigmoid(-z), 0.0)
    after = lax.cumsum(log_keep, axis=z.ndim - 1, reverse=True) - log_keep
    return jnp.where(mask, jnp.exp(jax.nn.log_sigmoid(z) + after), 0.0)


def stick_breaking_prompt(q, k, v, bias):
    b, t, h, dh = q.shape
    nb = t // Q_BLOCK
    scale = dh ** -0.5
    bias_f = bias.astype(F32)[None, :, None, None]
    q_blocks = jnp.moveaxis(q.reshape(b, nb, Q_BLOCK, h, dh), 1, 0)
    starts = jnp.arange(nb, dtype=jnp.int32) * Q_BLOCK
    kpos = jnp.arange(t, dtype=jnp.int32)

    def block(args):
        qb, start = args
        z = jnp.einsum('bqhd,bkhd->bhqk', qb, k, preferred_element_type=F32) * scale + bias_f
        qpos = start + jnp.arange(Q_BLOCK, dtype=jnp.int32)
        a = stick_breaking_weights(z, kpos[None, :] < qpos[:, None])
        return jnp.einsum('bhqk,bkhd->bqhd', a.astype(v.dtype), v, preferred_element_type=F32)

    o = lax.map(block, (q_blocks, starts))
    return jnp.moveaxis(o, 0, 1).reshape(b, t, h * dh).astype(q.dtype)


def stick_breaking_sample(q, k, v, bias, cache_k_l, cache_v_l, page_table):
    b, t, h, dh = q.shape
    past = page_table.shape[1] * PAGE_SIZE
    scale = dh ** -0.5
    bias_f = bias.astype(F32)[None, :, None, None]
    k_past = cache_k_l[page_table].reshape(b, past, h, dh)
    v_past = cache_v_l[page_table].reshape(b, past, h, dh)
    z = jnp.concatenate([
        jnp.einsum('bqhd,bkhd->bhqk', q, k_past, preferred_element_type=F32),
        jnp.einsum('bqhd,bkhd->bhqk', q, k, preferred_element_type=F32)], axis=-1) * scale + bias_f
    qpos = past + jnp.arange(t, dtype=jnp.int32)
    kpos = jnp.arange(past + t, dtype=jnp.int32)
    a = stick_breaking_weights(z, kpos[None, :] < qpos[:, None])
    o = (jnp.einsum('bhqk,bkhd->bqhd', a[..., :past].astype(v.dtype), v_past, preferred_element_type=F32)
         + jnp.einsum('bhqk,bkhd->bqhd', a[..., past:].astype(v.dtype), v, preferred_element_type=F32))
    return o.reshape(b, t, h * dh).astype(q.dtype)


def memory_kv(mem, g, w_k, w_v):
    b, m, _ = mem.shape
    mn = rmsnorm(mem, g)
    return ((mn @ w_k).reshape(b, m, MEM_HEADS, MEM_HEAD_DIM),
            (mn @ w_v).reshape(b, m, MEM_HEADS, MEM_HEAD_DIM))


def memory_attention(qm, mem_k, mem_v):
    b, t, _ = qm.shape
    q = qm.reshape(b, t, MEM_HEADS, MEM_HEAD_DIM)
    s = jnp.einsum('bthd,bmhd->bhtm', q, mem_k, preferred_element_type=F32) * (MEM_HEAD_DIM ** -0.5)
    p = jax.nn.softmax(s, axis=-1)
    o = jnp.einsum('bhtm,bmhd->bthd', p.astype(mem_v.dtype), mem_v, preferred_element_type=F32)
    return o.reshape(b, t, MEM_WIDTH).astype(qm.dtype)


def mixer_sublayer(x, mem_k, mem_v, h0_re, h0_im, attend, g_norm, w_in, sb_bias, ssm_params,
                   w_branch_ssm, w_branch_sb, w_branch_mem, w_out):
    b, t, _ = x.shape
    hn = rmsnorm(x, g_norm)
    u, q, k, v, qm, gate_logits = jnp.split(hn @ w_in, IN_SPLITS, axis=-1)
    o_ssm, h_re, h_im = s5_branch(u, h0_re, h0_im, *ssm_params)
    q = q.reshape(b, t, SB_HEADS, SB_HEAD_DIM)
    k = k.reshape(b, t, SB_HEADS, SB_HEAD_DIM)
    v = v.reshape(b, t, SB_HEADS, SB_HEAD_DIM)
    o_sb = attend(q, k, v, sb_bias)
    o_mem = memory_attention(qm, mem_k, mem_v)
    gates = jax.nn.sigmoid(gate_logits.astype(F32)).reshape(b, t, N_BRANCHES, D_MODEL)
    merged = (gates[:, :, 0] * (o_ssm @ w_branch_ssm).astype(F32)
              + gates[:, :, 1] * (o_sb @ w_branch_sb).astype(F32)
              + gates[:, :, 2] * (o_mem @ w_branch_mem).astype(F32))
    return x + merged.astype(x.dtype) @ w_out, k, v, h_re, h_im


def hierarchical_moe(x, g_norm, w_group_router, b_group_router, w_expert_router, b_expert_router,
                     w_exp_gate, w_exp_up, w_exp_down):
    b, t, _ = x.shape
    hn = rmsnorm(x, g_norm)
    hf = hn.astype(F32)
    group_prob = jax.nn.softmax(hf @ w_group_router.astype(F32) + b_group_router.astype(F32), axis=-1)
    g_val, g_idx = lax.top_k(group_prob, 1)
    g_onehot = jax.nn.one_hot(g_idx[..., 0], N_GROUPS, dtype=F32)
    exp_logits = (hf @ w_expert_router.astype(F32) + b_expert_router.astype(F32)).reshape(
        b, t, N_GROUPS, EXPERTS_PER_GROUP)
    sel_logits = jnp.sum(exp_logits * g_onehot[..., None], axis=2)
    e_val, e_idx = lax.top_k(jax.nn.softmax(sel_logits, axis=-1), TOP_K)
    e_val = e_val / jnp.sum(e_val, axis=-1, keepdims=True)
    within = jnp.sum(e_val[..., None] * jax.nn.one_hot(e_idx, EXPERTS_PER_GROUP, dtype=F32), axis=-2)
    gate = (g_val[..., None] * g_onehot[..., None] * within[..., None, :]).reshape(b, t, N_EXPERTS)
    a = jnp.einsum('btd,edf->btef', hn, w_exp_gate, preferred_element_type=F32)
    up = jnp.einsum('btd,edf->btef', hn, w_exp_up, preferred_element_type=F32)
    act = jax.nn.silu(a) * up * gate[..., None]
    out = jnp.einsum('btef,efd->btd', act.astype(x.dtype), w_exp_down, preferred_element_type=F32)
    return x + out.astype(x.dtype)


def setup_inputs(seed: int = 0) -> dict:
    key = jax.random.key(seed)
    keys = iter(jax.random.split(key, 64))

    def normal(shape, scale):
        return jax.random.normal(next(keys), shape, F32) * scale

    def gain(shape):
        return 1.0 + normal(shape, 0.02)

    n_pages = PAST_LEN // PAGE_SIZE
    n_used = DEC_BATCH * n_pages
    n_pool = n_used + n_used // 4
    page_table = jax.random.permutation(next(keys), n_pool)[:n_used].reshape(DEC_BATCH, n_pages).astype(jnp.int32)
    state_idx = jnp.arange(SSM_STATE, dtype=F32)
    head_frac = jnp.arange(SB_HEADS, dtype=F32) / (SB_HEADS - 1)
    return {
        'x_prompt': normal((BATCH, SEQ, D_MODEL), 1.0),
        'x_sample': normal((DEC_BATCH, DEC_SEQ, D_MODEL), 1.0),
        'mem_prompt': normal((BATCH, MEM_TOKENS, D_MODEL), 1.0),
        'cache_k': normal((DEPTH, n_pool, PAGE_SIZE, SB_HEADS, SB_HEAD_DIM), 1.0),
        'cache_v': normal((DEPTH, n_pool, PAGE_SIZE, SB_HEADS, SB_HEAD_DIM), 1.0),
        'page_table': page_table,
        'cache_mem_k': normal((DEPTH, DEC_BATCH, MEM_TOKENS, MEM_HEADS, MEM_HEAD_DIM), 1.0),
        'cache_mem_v': normal((DEPTH, DEC_BATCH, MEM_TOKENS, MEM_HEADS, MEM_HEAD_DIM), 1.0),
        'state_ssm_re': normal((DEPTH, DEC_BATCH, SSM_GROUPS, SSM_STATE), 0.5),
        'state_ssm_im': normal((DEPTH, DEC_BATCH, SSM_GROUPS, SSM_STATE), 0.5),
        'g_mix_norm': gain((DEPTH, D_MODEL)),
        'w_in': normal((DEPTH, D_MODEL, IN_WIDTH), D_MODEL ** -0.5),
        'sb_logit_bias': (SB_BIAS_NEAR + (SB_BIAS_FAR - SB_BIAS_NEAR) * head_frac)[None, :]
                         + normal((DEPTH, SB_HEADS), 0.1),
        'ssm_lambda_re': -0.5 + normal((DEPTH, SSM_GROUPS, SSM_STATE), 0.01),
        'ssm_lambda_im': math.pi * state_idx + normal((DEPTH, SSM_GROUPS, SSM_STATE), 0.01),
        'ssm_log_step': jax.random.uniform(next(keys), (DEPTH, SSM_GROUPS), F32,
                                           math.log(1e-3), math.log(1e-1)),
        'ssm_b_re': normal((DEPTH, SSM_GROUPS, SSM_STATE, SSM_GROUP), SSM_GROUP ** -0.5),
        'ssm_b_im': normal((DEPTH, SSM_GROUPS, SSM_STATE, SSM_GROUP), SSM_GROUP ** -0.5),
        'ssm_c_re': normal((DEPTH, SSM_GROUPS, SSM_GROUP, SSM_STATE), (2 * SSM_STATE) ** -0.5),
        'ssm_c_im': normal((DEPTH, SSM_GROUPS, SSM_GROUP, SSM_STATE), (2 * SSM_STATE) ** -0.5),
        'ssm_d': normal((DEPTH, SSM_GROUPS, SSM_GROUP), 1.0),
        'w_glu': normal((DEPTH, SSM_WIDTH, SSM_WIDTH), SSM_WIDTH ** -0.5),
        'b_glu': normal((DEPTH, SSM_WIDTH), 0.01),
        'g_mem_norm': gain((DEPTH, D_MODEL)),
        'w_mem_k': normal((DEPTH, D_MODEL, MEM_WIDTH), D_MODEL ** -0.5),
        'w_mem_v': normal((DEPTH, D_MODEL, MEM_WIDTH), D_MODEL ** -0.5),
        'w_branch_ssm': normal((DEPTH, SSM_WIDTH, D_MODEL), SSM_WIDTH ** -0.5),
        'w_branch_sb': normal((DEPTH, SB_WIDTH, D_MODEL), SB_WIDTH ** -0.5),
        'w_branch_mem': normal((DEPTH, MEM_WIDTH, D_MODEL), MEM_WIDTH ** -0.5),
        'w_out': normal((DEPTH, D_MODEL, D_MODEL), D_MODEL ** -0.5),
        'g_ffn_norm': gain((DEPTH, D_MODEL)),
        'w_group_router': normal((DEPTH, D_MODEL, N_GROUPS), D_MODEL ** -0.5),
        'b_group_router': normal((DEPTH, N_GROUPS), 0.01),
        'w_expert_router': normal((DEPTH, D_MODEL, N_EXPERTS), D_MODEL ** -0.5),
        'b_expert_router': normal((DEPTH, N_EXPERTS), 0.01),
        'w_exp_gate': normal((DEPTH, N_EXPERTS, D_MODEL, EXPERT_FF), D_MODEL ** -0.5),
        'w_exp_up': normal((DEPTH, N_EXPERTS, D_MODEL, EXPERT_FF), D_MODEL ** -0.5),
        'w_exp_down': normal((DEPTH, N_EXPERTS, EXPERT_FF, D_MODEL), EXPERT_FF ** -0.5),
        'g_final': gain((D_MODEL,)),
    }


def reference(x_prompt, x_sample, mem_prompt, cache_k, cache_v, page_table, cache_mem_k, cache_mem_v,
              state_ssm_re, state_ssm_im, g_mix_norm, w_in, sb_logit_bias, ssm_lambda_re, ssm_lambda_im,
              ssm_log_step, ssm_b_re, ssm_b_im, ssm_c_re, ssm_c_im, ssm_d, w_glu, b_glu, g_mem_norm, w_mem_k,
              w_mem_v, w_branch_ssm, w_branch_sb, w_branch_mem, w_out, g_ffn_norm, w_group_router,
              b_group_router, w_expert_router, b_expert_router, w_exp_gate, w_exp_up, w_exp_down, g_final):
    bp = x_prompt.shape[0]
    xp, xs = x_prompt, x_sample
    kp_l, vp_l, ks_l, vs_l = [], [], [], []
    hpr_l, hpi_l, hsr_l, hsi_l = [], [], [], []
    mkp_l, mvp_l = [], []
    for l in range(DEPTH):
        ssm_params = (ssm_lambda_re[l], ssm_lambda_im[l], ssm_log_step[l], ssm_b_re[l], ssm_b_im[l],
                      ssm_c_re[l], ssm_c_im[l], ssm_d[l], w_glu[l], b_glu[l])
        shared = (g_mix_norm[l], w_in[l], sb_logit_bias[l], ssm_params, w_branch_ssm[l], w_branch_sb[l],
                  w_branch_mem[l], w_out[l])
        moe = (g_ffn_norm[l], w_group_router[l], b_group_router[l], w_expert_router[l], b_expert_router[l],
               w_exp_gate[l], w_exp_up[l], w_exp_down[l])
        mk_p, mv_p = memory_kv(mem_prompt, g_mem_norm[l], w_mem_k[l], w_mem_v[l])
        h0 = jnp.zeros((bp, SSM_GROUPS, SSM_STATE), F32)
        xp, kp, vp, hpr, hpi = mixer_sublayer(xp, mk_p, mv_p, h0, h0, stick_breaking_prompt, *shared)
        xp = hierarchical_moe(xp, *moe)
        ck, cv = cache_k[l], cache_v[l]
        attend_sample = lambda q, k, v, bias: stick_breaking_sample(q, k, v, bias, ck, cv, page_table)
        xs, ks, vs, hsr, hsi = mixer_sublayer(xs, cache_mem_k[l], cache_mem_v[l], state_ssm_re[l],
                                              state_ssm_im[l], attend_sample, *shared)
        xs = hierarchical_moe(xs, *moe)
        kp_l.append(kp); vp_l.append(vp); ks_l.append(ks); vs_l.append(vs)
        hpr_l.append(hpr); hpi_l.append(hpi); hsr_l.append(hsr); hsi_l.append(hsi)
        mkp_l.append(mk_p); mvp_l.append(mv_p)
    y_prompt = rmsnorm(xp, g_final)
    y_sample = rmsnorm(xs, g_final)
    return (y_prompt, y_sample,
            jnp.stack(kp_l), jnp.stack(vp_l), jnp.stack(ks_l), jnp.stack(vs_l),
            jnp.stack(hpr_l), jnp.stack(hpi_l), jnp.stack(hsr_l), jnp.stack(hsi_l),
            jnp.stack(mkp_l), jnp.stack(mvp_l))
```

```python
import functools
import math

import jax
import jax.numpy as jnp
from jax import lax
from jax.experimental import pallas as pl
from jax.experimental.pallas import tpu as pltpu

F32 = jnp.float32
BF16 = jnp.bfloat16

V7X_VMEM_BYTES = 64 * 1024 * 1024
VMEM_LIMIT = V7X_VMEM_BYTES - 8 * 1024 * 1024
LANES = 128
SUBLANES = 8

SSM_GROUP = 16
SSM_STATE = 64
SB_HEAD_DIM = 128
MEM_HEADS = 4
PAGE_SIZE = 128
N_GROUPS = 4
EXPERTS_PER_GROUP = 8
N_EXPERTS = N_GROUPS * EXPERTS_PER_GROUP
RMS_EPS = 1e-6

SSM_BLOCK_GROUPS = 16
SSM_BLOCK_CH = SSM_BLOCK_GROUPS * SSM_GROUP
SSM_BLOCK_ST = SSM_BLOCK_GROUPS * SSM_STATE
SCAN_LANES = 512
ROUTER_EXPERT_LANE0 = 32


def _cparams(*sem):
    return pltpu.CompilerParams(dimension_semantics=sem, vmem_limit_bytes=VMEM_LIMIT)


def _sigmoid(x):
    return 1.0 / (1.0 + jnp.exp(-x))


def _tile(m, pref):
    t = min(m, pref)
    assert m % t == 0, (m, pref)
    return t


def _rms(x, g):
    return x * lax.rsqrt(jnp.mean(x * x, axis=-1, keepdims=True) + RMS_EPS) * g


def _rms_kernel(x_ref, g_ref, o_ref):
    o_ref[...] = _rms(x_ref[...], g_ref[...]).astype(o_ref.dtype)


def rmsnorm(x, g, out_dtype, tm=512):
    m, d = x.shape
    tm = _tile(m, tm)
    return pl.pallas_call(
        _rms_kernel,
        grid=(m // tm,),
        in_specs=[pl.BlockSpec((tm, d), lambda i: (i, 0)), pl.BlockSpec((1, d), lambda i: (0, 0))],
        out_specs=pl.BlockSpec((tm, d), lambda i: (i, 0)),
        out_shape=jax.ShapeDtypeStruct((m, d), out_dtype),
        compiler_params=_cparams("parallel"),
        name="rmsnorm",
    )(x, g.reshape(1, d))


def _mm_kernel(*refs, has_res):
    a_ref, b_ref = refs[0], refs[1]
    o_refs = refs[3:] if has_res else refs[2:]
    acc = jnp.dot(a_ref[...], b_ref[...], preferred_element_type=F32)
    if has_res:
        acc = refs[2][...] + acc
    for o_ref in o_refs:
        o_ref[...] = acc.astype(o_ref.dtype)


def matmul(a, b, out_dtypes, res=None, tm=512, tn=1024, name="matmul"):
    m, k = a.shape
    n = b.shape[1]
    tm, tn = _tile(m, tm), _tile(n, tn)
    in_specs = [pl.BlockSpec((tm, k), lambda i, j: (i, 0)), pl.BlockSpec((k, tn), lambda i, j: (0, j))]
    args = [a, b]
    if res is not None:
        in_specs.append(pl.BlockSpec((tm, tn), lambda i, j: (i, j)))
        args.append(res)
    outs = pl.pallas_call(
        functools.partial(_mm_kernel, has_res=res is not None),
        grid=(m // tm, n // tn),
        in_specs=in_specs,
        out_specs=[pl.BlockSpec((tm, tn), lambda i, j: (i, j)) for _ in out_dtypes],
        out_shape=[jax.ShapeDtypeStruct((m, n), dt) for dt in out_dtypes],
        compiler_params=_cparams("parallel", "arbitrary"),
        name=name,
    )(*args)
    return outs


def _proj_kernel(a_ref, b_ref, u_ref, q_ref, k_ref, kb_ref, v_ref, vb_ref, qm_ref):
    j = pl.program_id(1)
    acc = jnp.dot(a_ref[...], b_ref[...], preferred_element_type=F32)

    @pl.when(j == 0)
    def _():
        u_ref[...] = acc

    @pl.when(j == 1)
    def _():
        q_ref[...] = acc.astype(BF16)

    @pl.when(j == 2)
    def _():
        k_ref[...] = acc
        kb_ref[...] = acc.astype(BF16)

    @pl.when(j == 3)
    def _():
        v_ref[...] = acc
        vb_ref[...] = acc.astype(BF16)

    @pl.when(j == 4)
    def _():
        qm_ref[...] = acc.astype(BF16)


def in_projection(hn, w5, width, tm=512):
    m, k = hn.shape
    tm = _tile(m, tm)
    dts = (F32, BF16, F32, BF16, F32, BF16, BF16)
    return pl.pallas_call(
        _proj_kernel,
        grid=(m // tm, 5),
        in_specs=[pl.BlockSpec((tm, k), lambda i, j: (i, 0)), pl.BlockSpec((k, width), lambda i, j: (0, j))],
        out_specs=[pl.BlockSpec((tm, width), lambda i, j: (i, 0)) for _ in dts],
        out_shape=[jax.ShapeDtypeStruct((m, width), dt) for dt in dts],
        compiler_params=_cparams("parallel", "arbitrary"),
        name="in_projection",
    )(hn, w5)


def _cmul_add(ar, ai, hr, hi, xr, xi):
    return ar * hr - ai * hi + xr, ar * hi + ai * hr + xi


def _s5_readout(q, x_ref, u_ref, cblk_ref, d_ref, z_ref):
    cols = slice(q * SSM_BLOCK_CH, (q + 1) * SSM_BLOCK_CH)
    y = jnp.dot(x_ref[...].astype(BF16), cblk_ref[q], preferred_element_type=F32)
    y = y + d_ref[:, cols] * u_ref[:, cols]
    z_ref[:, cols] = jax.nn.gelu(y)


def _s5_glu(z_ref, wglu_ref, bglu_ref):
    z = z_ref[...]
    gate = _sigmoid(jnp.dot(z.astype(BF16), wglu_ref[...], preferred_element_type=F32) + bglu_ref[...])
    return z * gate


def _s5_prompt_kernel(u_ref, h0r_ref, h0i_ref, bblk_ref, cblk_ref, d_ref, apr_ref, api_ref, wglu_ref, bglu_ref,
                      o_ref, hr_out_ref, hi_out_ref, x_ref, z_ref, hcr_ref, hci_ref, *, nj):
    n_blocks = bblk_ref.shape[0]

    @pl.when(pl.program_id(0) == 0)
    def _():
        hcr_ref[...] = h0r_ref[...]
        hci_ref[...] = h0i_ref[...]

    for q in range(n_blocks):
        ub = u_ref[:, q * SSM_BLOCK_CH:(q + 1) * SSM_BLOCK_CH]
        x_ref[...] = jnp.dot(ub.astype(BF16), bblk_ref[q], preferred_element_type=F32)
        for c in range(SSM_BLOCK_ST // SCAN_LANES):
            re = slice(c * SCAN_LANES, (c + 1) * SCAN_LANES)
            im = slice(SSM_BLOCK_ST + c * SCAN_LANES, SSM_BLOCK_ST + (c + 1) * SCAN_LANES)
            st = slice(q * SSM_BLOCK_ST + c * SCAN_LANES, q * SSM_BLOCK_ST + (c + 1) * SCAN_LANES)
            ar = jnp.broadcast_to(apr_ref[0:1, st], (SUBLANES, SCAN_LANES))
            ai = jnp.broadcast_to(api_ref[0:1, st], (SUBLANES, SCAN_LANES))

            def scan_body(j, carry, re=re, im=im, ar=ar, ai=ai):
                rows = pl.ds(pl.multiple_of(j * SUBLANES, SUBLANES), SUBLANES)
                hr, hi = _cmul_add(ar, ai, carry[0], carry[1], x_ref[rows, re], x_ref[rows, im])
                x_ref[rows, re] = hr
                x_ref[rows, im] = hi
                return hr, hi

            zero = jnp.zeros((SUBLANES, SCAN_LANES), F32)
            er, ei = lax.fori_loop(0, nj, scan_body, (zero, zero), unroll=2)

            alr, ali = apr_ref[nj - 1:nj, st], api_ref[nj - 1:nj, st]
            cr, ci = hcr_ref[:, st], hci_ref[:, st]
            crs, cis = [], []
            for s in range(SUBLANES):
                crs.append(cr)
                cis.append(ci)
                cr, ci = _cmul_add(alr, ali, cr, ci, er[s:s + 1, :], ei[s:s + 1, :])
            hcr_ref[:, st] = cr
            hci_ref[:, st] = ci
            cin_r = jnp.concatenate(crs, axis=0)
            cin_i = jnp.concatenate(cis, axis=0)

            def fix_body(j, _, re=re, im=im, st=st, cin_r=cin_r, cin_i=cin_i):
                rows = pl.ds(pl.multiple_of(j * SUBLANES, SUBLANES), SUBLANES)
                pr = jnp.broadcast_to(apr_ref[pl.ds(j, 1), st], (SUBLANES, SCAN_LANES))
                pi = jnp.broadcast_to(api_ref[pl.ds(j, 1), st], (SUBLANES, SCAN_LANES))
                hr, hi = _cmul_add(pr, pi, cin_r, cin_i, x_ref[rows, re], x_ref[rows, im])
                x_ref[rows, re] = hr
                x_ref[rows, im] = hi
                return 0

            lax.fori_loop(0, nj, fix_body, 0, unroll=2)
        _s5_readout(q, x_ref, u_ref, cblk_ref, d_ref, z_ref)

    o_ref[...] = _s5_glu(z_ref, wglu_ref, bglu_ref).astype(o_ref.dtype)
    hr_out_ref[...] = hcr_ref[...]
    hi_out_ref[...] = hci_ref[...]


def _s5_sample_kernel(u_ref, h0r_ref, h0i_ref, bblk_ref, cblk_ref, d_ref, apr_ref, api_ref, wglu_ref, bglu_ref,
                      o_ref, hr_out_ref, hi_out_ref, x_ref, z_ref, *, nb, nt):
    n_blocks = bblk_ref.shape[0]
    for q in range(n_blocks):
        ub = u_ref[:, q * SSM_BLOCK_CH:(q + 1) * SSM_BLOCK_CH]
        x_ref[...] = jnp.dot(ub.astype(BF16), bblk_ref[q], preferred_element_type=F32)
        for c in range(SSM_BLOCK_ST // SCAN_LANES):
            re = slice(c * SCAN_LANES, (c + 1) * SCAN_LANES)
            im = slice(SSM_BLOCK_ST + c * SCAN_LANES, SSM_BLOCK_ST + (c + 1) * SCAN_LANES)
            st = slice(q * SSM_BLOCK_ST + c * SCAN_LANES, q * SSM_BLOCK_ST + (c + 1) * SCAN_LANES)
            ar, ai = apr_ref[0:1, st], api_ref[0:1, st]
            hr, hi = h0r_ref[:, st], h0i_ref[:, st]
            for t in range(nt):
                rows = slice(t * nb, (t + 1) * nb)
                hr, hi = _cmul_add(ar, ai, hr, hi, x_ref[rows, re], x_ref[rows, im])
                x_ref[rows, re] = hr
                x_ref[rows, im] = hi
            hr_out_ref[:, st] = hr
            hi_out_ref[:, st] = hi
        _s5_readout(q, x_ref, u_ref, cblk_ref, d_ref, z_ref)
    o_ref[...] = _s5_glu(z_ref, wglu_ref, bglu_ref).astype(o_ref.dtype)


def _s5_params(lam_re, lam_im, log_step, b_re, b_im, c_re, c_im, d_skip, n_pow):
    g, p = lam_re.shape
    dt = jnp.exp(log_step)[:, None]
    steps = jnp.arange(1, n_pow + 1, dtype=F32)[:, None, None]
    mag = jnp.exp(lam_re * dt * steps)
    phase = lam_im * dt * steps
    ap_re = (mag * jnp.cos(phase)).reshape(n_pow, g * p)
    ap_im = (mag * jnp.sin(phase)).reshape(n_pow, g * p)
    a_re, a_im = mag[0] * jnp.cos(phase[0]), mag[0] * jnp.sin(phase[0])
    num_re = a_re - 1.0
    den = lam_re * lam_re + lam_im * lam_im
    k_re = (num_re * lam_re + a_im * lam_im) / den
    k_im = (a_im * lam_re - num_re * lam_im) / den
    bp_re = k_re[..., None] * b_re - k_im[..., None] * b_im
    bp_im = k_re[..., None] * b_im + k_im[..., None] * b_re
    nblk = g // SSM_BLOCK_GROUPS
    eye = jnp.eye(SSM_BLOCK_GROUPS, dtype=F32)

    def b_block(bp):
        bp = bp.reshape(nblk, SSM_BLOCK_GROUPS, p, SSM_GROUP)
        return jnp.einsum('qgpc,gh->qgchp', bp, eye).reshape(nblk, SSM_BLOCK_CH, SSM_BLOCK_ST)

    def c_block(c):
        c = c.reshape(nblk, SSM_BLOCK_GROUPS, SSM_GROUP, p)
        return jnp.einsum('qgcp,gh->qgphc', c, eye).reshape(nblk, SSM_BLOCK_ST, SSM_BLOCK_CH)

    bblk = jnp.concatenate([b_block(bp_re), b_block(bp_im)], axis=2).astype(BF16)
    cblk = jnp.concatenate([c_block(c_re), -c_block(c_im)], axis=1).astype(BF16)
    return bblk, cblk, d_skip.reshape(1, g * SSM_GROUP), ap_re, ap_im


def _const_spec(shape):
    return pl.BlockSpec(shape, lambda *_: (0,) * len(shape))


def s5_prompt(u, h0r, h0i, params, w_glu, b_glu, chunk):
    t, w = u.shape
    nj = chunk // SUBLANES
    bblk, cblk, d, apr, api = params
    ns = apr.shape[1]
    consts = [h0r, h0i, bblk, cblk, d, apr, api, w_glu, b_glu.reshape(1, w)]
    u_perm = u.reshape(t // chunk, SUBLANES, nj, w).transpose(0, 2, 1, 3).reshape(t, w)
    o_perm, h_re, h_im = pl.pallas_call(
        functools.partial(_s5_prompt_kernel, nj=nj),
        grid=(t // chunk,),
        in_specs=[pl.BlockSpec((chunk, w), lambda i: (i, 0))] + [_const_spec(c.shape) for c in consts],
        out_specs=[pl.BlockSpec((chunk, w), lambda i: (i, 0)), _const_spec((1, ns)), _const_spec((1, ns))],
        out_shape=[jax.ShapeDtypeStruct((t, w), BF16), jax.ShapeDtypeStruct((1, ns), F32),
                   jax.ShapeDtypeStruct((1, ns), F32)],
        scratch_shapes=[pltpu.VMEM((chunk, 2 * SSM_BLOCK_ST), F32), pltpu.VMEM((chunk, w), F32),
                        pltpu.VMEM((1, ns), F32), pltpu.VMEM((1, ns), F32)],
        compiler_params=_cparams("arbitrary"),
        name="s5_prompt",
    )(u_perm, *consts)
    o = o_perm.reshape(t // chunk, nj, SUBLANES, w).transpose(0, 2, 1, 3).reshape(t, w)
    return o, h_re, h_im


def s5_sample(u_tb, h0r, h0i, params, w_glu, b_glu, nb, nt):
    m, w = u_tb.shape
    bblk, cblk, d, apr, api = params
    ns = apr.shape[1]
    consts = [h0r, h0i, bblk, cblk, d, apr, api, w_glu, b_glu.reshape(1, w)]
    return pl.pallas_call(
        functools.partial(_s5_sample_kernel, nb=nb, nt=nt),
        grid=(1,),
        in_specs=[_const_spec(u_tb.shape)] + [_const_spec(c.shape) for c in consts],
        out_specs=[_const_spec((m, w)), _const_spec((nb, ns)), _const_spec((nb, ns))],
        out_shape=[jax.ShapeDtypeStruct((m, w), BF16), jax.ShapeDtypeStruct((nb, ns), F32),
                   jax.ShapeDtypeStruct((nb, ns), F32)],
        scratch_shapes=[pltpu.VMEM((m, 2 * SSM_BLOCK_ST), F32), pltpu.VMEM((m, w), F32)],
        compiler_params=_cparams("arbitrary"),
        name="s5_sample",
    )(u_tb, *consts)


def _log_sigmoid_pair(z):
    l1p = jnp.log1p(jnp.exp(-jnp.abs(z)))
    return jnp.minimum(z, 0.0) - l1p, -jnp.maximum(z, 0.0) - l1p


def _sb_prompt_kernel(bias_ref, q_ref, k_ref, v_ref, o_ref, acc_ref, carry_ref, *, tq, scale):
    h, i = pl.program_id(0), pl.program_id(1)
    bias = bias_ref[h]
    q = q_ref[...]
    row = lax.broadcasted_iota(jnp.int32, (tq, tq), 0)
    col = lax.broadcasted_iota(jnp.int32, (tq, tq), 1)
    newer = (row > col).astype(BF16)
    causal = col < row
    acc_ref[...] = jnp.zeros_like(acc_ref)
    carry_ref[...] = jnp.zeros_like(carry_ref)

    def tile(kb, masked):
        rows = pl.ds(pl.multiple_of(kb * tq, tq), tq)
        z = lax.dot_general(q, k_ref[rows, :], (((1,), (1,)), ((), ())), preferred_element_type=F32) * scale + bias
        ls, lk = _log_sigmoid_pair(z)
        if masked:
            lk = jnp.where(causal, lk, 0.0)
        after = jnp.dot(lk.astype(BF16), newer, preferred_element_type=F32) + carry_ref[...]
        a = jnp.exp(ls + after)
        if masked:
            a = jnp.where(causal, a, 0.0)
        acc_ref[...] += jnp.dot(a.astype(BF16), v_ref[rows, :], preferred_element_type=F32)
        carry_ref[...] += jnp.sum(lk, axis=1, keepdims=True)

    tile(i, True)

    def body(jj, _):
        tile(i - 1 - jj, False)
        return 0

    lax.fori_loop(0, i, body, 0)
    o_ref[...] = acc_ref[...].astype(o_ref.dtype)


def sb_attention_prompt(q, k, v, bias, tq=256):
    t, w = q.shape
    n_heads = w // SB_HEAD_DIM
    tq = _tile(t, tq)
    grid_spec = pltpu.PrefetchScalarGridSpec(
        num_scalar_prefetch=1,
        grid=(n_heads, t // tq),
        in_specs=[pl.BlockSpec((tq, SB_HEAD_DIM), lambda h, i, b: (i, h)),
                  pl.BlockSpec((t, SB_HEAD_DIM), lambda h, i, b: (0, h)),
                  pl.BlockSpec((t, SB_HEAD_DIM), lambda h, i, b: (0, h))],
        out_specs=pl.BlockSpec((tq, SB_HEAD_DIM), lambda h, i, b: (i, h)),
        scratch_shapes=[pltpu.VMEM((tq, SB_HEAD_DIM), F32), pltpu.VMEM((tq, 1), F32)],
    )
    return pl.pallas_call(
        functools.partial(_sb_prompt_kernel, tq=tq, scale=SB_HEAD_DIM ** -0.5),
        grid_spec=grid_spec,
        out_shape=jax.ShapeDtypeStruct((t, w), BF16),
        compiler_params=_cparams("parallel", "arbitrary"),
        name="sb_attention_prompt",
    )(bias, q, k, v)


def _sb_sample_kernel(pt_ref, *refs, n_pp, scale):
    wq_ref, bias_ref, kn_ref, vn_ref = refs[:4]
    kp_refs = refs[4:4 + n_pp]
    vp_refs = refs[4 + n_pp:4 + 2 * n_pp]
    o_ref, acc_ref, carry_ref = refs[4 + 2 * n_pp:]
    p = pl.program_id(1)
    tk = kn_ref.shape[0]
    nq = o_ref.shape[0]
    row = lax.broadcasted_iota(jnp.int32, (tk, tk), 0)
    col = lax.broadcasted_iota(jnp.int32, (tk, tk), 1)
    newer = (col > row).astype(BF16)

    def tile(kt, vt, mask):
        z = jnp.dot(kt, wq_ref[...], preferred_element_type=F32) * scale + bias_ref[...]
        ls, lk = _log_sigmoid_pair(z)
        if mask is not None:
            lk = jnp.where(mask, lk, 0.0)
        after = jnp.dot(newer, lk.astype(BF16), preferred_element_type=F32) + carry_ref[...]
        a = jnp.exp(ls + after)
        if mask is not None:
            a = jnp.where(mask, a, 0.0)
        acc_ref[...] += jnp.dot(a.T.astype(BF16), vt, preferred_element_type=F32)
        carry_ref[...] += jnp.sum(lk, axis=0, keepdims=True)

    @pl.when(p == 0)
    def _():
        acc_ref[...] = jnp.zeros_like(acc_ref)
        carry_ref[...] = jnp.zeros_like(carry_ref)
        lane_q = lax.broadcasted_iota(jnp.int32, (tk, LANES), 1) % nq
        key = lax.broadcasted_iota(jnp.int32, (tk, LANES), 0)
        tile(kn_ref[...], vn_ref[...], key < lane_q)

    for r in range(n_pp):
        tile(kp_refs[r][...].astype(BF16), vp_refs[r][...].astype(BF16), None)

    @pl.when(p == pl.num_programs(1) - 1)
    def _():
        for h in range(o_ref.shape[1] // SB_HEAD_DIM):
            cols = slice(h * SB_HEAD_DIM, (h + 1) * SB_HEAD_DIM)
            o_ref[:, cols] = acc_ref[h * nq:(h + 1) * nq, cols]


def sb_attention_sample(q, k_new, v_new, bias, cache_k, cache_v, page_table, pages_per_step=4):
    nb, nq, w = q.shape
    n_heads = w // SB_HEAD_DIM
    n_pages = page_table.shape[1]
    n_pp = pages_per_step
    assert n_pages % n_pp == 0 and n_heads * nq <= LANES and nq <= PAGE_SIZE
    q4 = q.reshape(nb, nq, n_heads, SB_HEAD_DIM).transpose(0, 2, 3, 1)
    wq = jnp.einsum('bhdt,hg->bhdgt', q4, jnp.eye(n_heads, dtype=q.dtype)).reshape(nb, w, n_heads * nq)
    wq = jnp.pad(wq, ((0, 0), (0, 0), (0, LANES - n_heads * nq)))
    bias_row = jnp.pad(jnp.repeat(bias, nq), (0, LANES - n_heads * nq)).reshape(1, LANES)
    pad = ((0, 0), (0, PAGE_SIZE - nq), (0, 0))
    k_new, v_new = jnp.pad(k_new, pad), jnp.pad(v_new, pad)

    def page_spec(r):
        return pl.BlockSpec((None, PAGE_SIZE, w), lambda b, p, pt: (pt[b, n_pages - 1 - (p * n_pp + r)], 0, 0))

    per_b = lambda b, p, pt: (b, 0, 0)
    grid_spec = pltpu.PrefetchScalarGridSpec(
        num_scalar_prefetch=1,
        grid=(nb, n_pages // n_pp),
        in_specs=[pl.BlockSpec((None, w, LANES), per_b), pl.BlockSpec((1, LANES), lambda b, p, pt: (0, 0)),
                  pl.BlockSpec((None, PAGE_SIZE, w), per_b), pl.BlockSpec((None, PAGE_SIZE, w), per_b)]
                 + [page_spec(r) for r in range(n_pp)] + [page_spec(r) for r in range(n_pp)],
        out_specs=pl.BlockSpec((None, nq, w), per_b),
        scratch_shapes=[pltpu.VMEM((LANES, w), F32), pltpu.VMEM((1, LANES), F32)],
    )
    return pl.pallas_call(
        functools.partial(_sb_sample_kernel, n_pp=n_pp, scale=SB_HEAD_DIM ** -0.5),
        grid_spec=grid_spec,
        out_shape=jax.ShapeDtypeStruct((nb, nq, w), F32),
        compiler_params=_cparams("parallel", "arbitrary"),
        name="sb_attention_sample",
    )(page_table, wq, bias_row, k_new, v_new, *([cache_k] * n_pp), *([cache_v] * n_pp))


def _mem_attn_kernel(q_ref, k_ref, v_ref, o_ref, *, scale):
    dh = o_ref.shape[1] // MEM_HEADS
    for h in range(MEM_HEADS):
        cols = slice(h * dh, (h + 1) * dh)
        kh = k_ref[:, cols].astype(BF16)
        vh = v_ref[:, cols].astype(BF16)
        s = lax.dot_general(q_ref[:, cols], kh, (((1,), (1,)), ((), ())), preferred_element_type=F32) * scale
        e = jnp.exp(s - jnp.max(s, axis=-1, keepdims=True))
        p = e / jnp.sum(e, axis=-1, keepdims=True)
        o_ref[:, cols] = jnp.dot(p.astype(BF16), vh, preferred_element_type=F32).astype(o_ref.dtype)


def memory_attention(q, mem_k, mem_v, tq=512):
    nb, t, w = q.shape
    n_mem = mem_k.shape[1]
    tq = _tile(t, tq)
    return pl.pallas_call(
        functools.partial(_mem_attn_kernel, scale=(w // MEM_HEADS) ** -0.5),
        grid=(nb, t // tq),
        in_specs=[pl.BlockSpec((None, tq, w), lambda b, i: (b, i, 0)),
                  pl.BlockSpec((None, n_mem, w), lambda b, i: (b, 0, 0)),
                  pl.BlockSpec((None, n_mem, w), lambda b, i: (b, 0, 0))],
        out_specs=pl.BlockSpec((None, tq, w), lambda b, i: (b, i, 0)),
        out_shape=jax.ShapeDtypeStruct((nb, t, w), BF16),
        compiler_params=_cparams("parallel", "arbitrary"),
        name="memory_attention",
    )(q, mem_k, mem_v)


def _merge_kernel(hn_ref, o0_ref, o1_ref, o2_ref, g0_ref, g1_ref, g2_ref, b0_ref, b1_ref, b2_ref, out_ref):
    hn = hn_ref[...]
    total = None
    for o_ref, g_ref, b_ref in ((o0_ref, g0_ref, b0_ref), (o1_ref, g1_ref, b1_ref), (o2_ref, g2_ref, b2_ref)):
        gate = _sigmoid(jnp.dot(hn, g_ref[...], preferred_element_type=F32))
        term = gate * jnp.dot(o_ref[...], b_ref[...], preferred_element_type=F32)
        total = term if total is None else total + term
    out_ref[...] = total.astype(out_ref.dtype)


def gated_merge(hn, branches, w_gates, w_branches, tm=512, tn=512):
    m, d = hn.shape
    tm, tn = _tile(m, tm), _tile(d, tn)
    nj = d // tn
    row = lambda i, j: (i, 0)
    in_specs = [pl.BlockSpec((tm, d), row)]
    in_specs += [pl.BlockSpec((tm, o.shape[1]), row) for o in branches]
    in_specs += [pl.BlockSpec((d, tn), functools.partial(lambda i, j, b: (0, b * nj + j), b=b)) for b in range(3)]
    in_specs += [pl.BlockSpec((w.shape[0], tn), lambda i, j: (0, j)) for w in w_branches]
    return pl.pallas_call(
        _merge_kernel,
        grid=(m // tm, nj),
        in_specs=in_specs,
        out_specs=pl.BlockSpec((tm, tn), lambda i, j: (i, j)),
        out_shape=jax.ShapeDtypeStruct((m, d), BF16),
        compiler_params=_cparams("parallel", "arbitrary"),
        name="gated_merge",
    )(hn, *branches, w_gates, w_gates, w_gates, *w_branches)


def _router_kernel(x_ref, g_ref, wr_ref, br_ref, hn_ref, gate_ref):
    hn = _rms(x_ref[...], g_ref[...])
    hn_ref[...] = hn.astype(hn_ref.dtype)
    logits = jnp.dot(hn, wr_ref[...], preferred_element_type=F32, precision=lax.Precision.HIGHEST) + br_ref[...]
    lane = lax.broadcasted_iota(jnp.int32, logits.shape, 1).astype(F32)
    neg, far = -1e30, 1e9
    rmax = lambda v: jnp.max(v, axis=-1, keepdims=True)
    rmin = lambda v: jnp.min(v, axis=-1, keepdims=True)
    rsum = lambda v: jnp.sum(v, axis=-1, keepdims=True)
    is_group = lane < N_GROUPS
    gl = jnp.where(is_group, logits, neg)
    gmax = rmax(gl)
    g_val = 1.0 / rsum(jnp.where(is_group, jnp.exp(gl - gmax), 0.0))
    g_idx = rmin(jnp.where(gl == gmax, lane, far))
    lo = ROUTER_EXPERT_LANE0 + EXPERTS_PER_GROUP * g_idx
    in_group = jnp.logical_and(lane >= lo, lane < lo + EXPERTS_PER_GROUP)
    el = jnp.where(in_group, logits, neg)
    ex = jnp.where(in_group, jnp.exp(el - rmax(el)), 0.0)
    prob = jnp.where(in_group, ex / rsum(ex), -1.0)
    v1 = rmax(prob)
    i1 = rmin(jnp.where(prob == v1, lane, far))
    rest = jnp.where(lane == i1, -1.0, prob)
    v2 = rmax(rest)
    i2 = rmin(jnp.where(rest == v2, lane, far))
    within = jnp.where(lane == i1, v1 / (v1 + v2), jnp.where(lane == i2, v2 / (v1 + v2), 0.0))
    gate_ref[...] = g_val * within


def moe_router(x, g, w_router, b_router, tm=256):
    m, d = x.shape
    tm = _tile(m, tm)
    return pl.pallas_call(
        _router_kernel,
        grid=(m // tm,),
        in_specs=[pl.BlockSpec((tm, d), lambda i: (i, 0)), _const_spec((1, d)), _const_spec((d, LANES)),
                  _const_spec((1, LANES))],
        out_specs=[pl.BlockSpec((tm, d), lambda i: (i, 0)), pl.BlockSpec((tm, LANES), lambda i: (i, 0))],
        out_shape=[jax.ShapeDtypeStruct((m, d), BF16), jax.ShapeDtypeStruct((m, LANES), F32)],
        compiler_params=_cparams("parallel"),
        name="moe_router",
    )(x, g.reshape(1, d), w_router, b_router)


def _experts_kernel(hn_ref, gate_ref, x_ref, wgu_ref, wd_ref, gf_ref, o_ref):
    e = pl.program_id(1)
    ff = wd_ref.shape[0]

    @pl.when(e == 0)
    def _():
        o_ref[...] = x_ref[...]

    lane = lax.broadcasted_iota(jnp.int32, gate_ref.shape, 1)
    gate = jnp.sum(jnp.where(lane == e + ROUTER_EXPERT_LANE0, gate_ref[...], 0.0), axis=-1, keepdims=True)
    au = jnp.dot(hn_ref[...], wgu_ref[...], preferred_element_type=F32)
    a, up = au[:, :ff], au[:, ff:]
    act = a * _sigmoid(a) * up * gate
    o_ref[...] += jnp.dot(act.astype(BF16), wd_ref[...], preferred_element_type=F32)

    @pl.when(e == pl.num_programs(1) - 1)
    def _():
        o_ref[...] = _rms(o_ref[...], gf_ref[...])


def moe_experts_final_norm(hn, gate, x, w_gate_up, w_down, g_final, tm=1024):
    m, d = x.shape
    n_exp, _, ff2 = w_gate_up.shape
    tm = _tile(m, tm)
    row = lambda i, e: (i, 0)
    return pl.pallas_call(
        _experts_kernel,
        grid=(m // tm, n_exp),
        in_specs=[pl.BlockSpec((tm, d), row), pl.BlockSpec((tm, LANES), row), pl.BlockSpec((tm, d), row),
                  pl.BlockSpec((None, d, ff2), lambda i, e: (e, 0, 0)),
                  pl.BlockSpec((None, ff2 // 2, d), lambda i, e: (e, 0, 0)),
                  pl.BlockSpec((1, d), lambda i, e: (0, 0))],
        out_specs=pl.BlockSpec((tm, d), row),
        out_shape=jax.ShapeDtypeStruct((m, d), F32),
        compiler_params=_cparams("parallel", "arbitrary"),
        name="moe_experts",
    )(hn, gate, x, w_gate_up, w_down, g_final.reshape(1, d))


S5_CHUNK = 512


def kernel(x_prompt, x_sample, mem_prompt, cache_k, cache_v, page_table, cache_mem_k, cache_mem_v, state_ssm_re, state_ssm_im, g_mix_norm, w_in, sb_logit_bias, ssm_lambda_re, ssm_lambda_im, ssm_log_step, ssm_b_re, ssm_b_im, ssm_c_re, ssm_c_im, ssm_d, w_glu, b_glu, g_mem_norm, w_mem_k, w_mem_v, w_branch_ssm, w_branch_sb, w_branch_mem, w_out, g_ffn_norm, w_group_router, b_group_router, w_expert_router, b_expert_router, w_exp_gate, w_exp_up, w_exp_down, g_final):
    depth = w_in.shape[0]
    assert depth == 1, "one mixer + MoE layer per step"
    bp, tp, d = x_prompt.shape
    bs, ts, _ = x_sample.shape
    assert bp == 1
    n_groups, n_state = ssm_lambda_re.shape[1:]
    ssm_w = n_groups * SSM_GROUP
    n_st = n_groups * n_state
    sb_w = cache_k.shape[3] * cache_k.shape[4]
    mem_w = cache_mem_k.shape[3] * cache_mem_k.shape[4]
    n_mem = mem_prompt.shape[1]
    assert ssm_w == sb_w == mem_w
    l = 0

    w5 = w_in[l, :, :5 * ssm_w].astype(BF16)
    w_gates = w_in[l, :, 5 * ssm_w:].astype(BF16)
    w_branches = [w_branch_ssm[l].astype(BF16), w_branch_sb[l].astype(BF16), w_branch_mem[l].astype(BF16)]
    w_out_b = w_out[l].astype(BF16)
    w_glu_b = w_glu[l].astype(BF16)
    w_memkv = jnp.concatenate([w_mem_k[l], w_mem_v[l]], axis=1).astype(BF16)
    w_gate_up = jnp.concatenate([w_exp_gate[l], w_exp_up[l]], axis=2).astype(BF16)
    w_down = w_exp_down[l].astype(BF16)
    w_router = jnp.zeros((d, LANES), F32)
    w_router = w_router.at[:, :N_GROUPS].set(w_group_router[l])
    w_router = w_router.at[:, ROUTER_EXPERT_LANE0:ROUTER_EXPERT_LANE0 + N_EXPERTS].set(w_expert_router[l])
    b_router = jnp.zeros((1, LANES), F32)
    b_router = b_router.at[0, :N_GROUPS].set(b_group_router[l])
    b_router = b_router.at[0, ROUTER_EXPERT_LANE0:ROUTER_EXPERT_LANE0 + N_EXPERTS].set(b_expert_router[l])
    s5p = _s5_params(ssm_lambda_re[l], ssm_lambda_im[l], ssm_log_step[l], ssm_b_re[l], ssm_b_im[l],
                     ssm_c_re[l], ssm_c_im[l], ssm_d[l], S5_CHUNK // SUBLANES)
    bias = sb_logit_bias[l]

    mn = rmsnorm(mem_prompt.reshape(n_mem, d), g_mem_norm[l], BF16)
    (mem_kv,) = matmul(mn, w_memkv, (F32,), tm=n_mem, tn=1024, name="memory_kv")
    mk_p, mv_p = mem_kv[:, :mem_w], mem_kv[:, mem_w:]

    def mixer_and_moe(x, attend, mem_k, mem_v, s5):
        hn = rmsnorm(x, g_mix_norm[l], BF16)
        u, q, k, kb, v, vb, qm = in_projection(hn, w5, ssm_w)
        o_ssm, h_re, h_im = s5(u)
        o_sb = attend(q, kb, vb)
        o_mem = memory_attention(qm.reshape(mem_k.shape[0], -1, mem_w), mem_k, mem_v).reshape(-1, mem_w)
        merged = gated_merge(hn, (o_ssm, o_sb, o_mem), w_gates, w_branches)
        (x2,) = matmul(merged, w_out_b, (F32,), res=x, tm=512, tn=1024, name="out_projection")
        hn2, gate = moe_router(x2, g_ffn_norm[l], w_router, b_router)
        y = moe_experts_final_norm(hn2, gate, x2, w_gate_up, w_down, g_final)
        return y, k, v, h_re, h_im

    zeros = jnp.zeros((1, n_st), F32)
    y_p, k_p, v_p, hr_p, hi_p = mixer_and_moe(
        x_prompt.reshape(tp, d),
        lambda q, kb, vb: sb_attention_prompt(q, kb, vb, bias),
        mk_p.reshape(1, n_mem, mem_w), mv_p.reshape(1, n_mem, mem_w),
        lambda u: s5_prompt(u, zeros, zeros, s5p, w_glu_b, b_glu[l], S5_CHUNK))

    def s5_s(u):
        u_tb = u.reshape(bs, ts, ssm_w).transpose(1, 0, 2).reshape(ts * bs, ssm_w)
        o_tb, h_re, h_im = s5_sample(u_tb, state_ssm_re[l].reshape(bs, n_st), state_ssm_im[l].reshape(bs, n_st),
                                     s5p, w_glu_b, b_glu[l], bs, ts)
        return o_tb.reshape(ts, bs, ssm_w).transpose(1, 0, 2).reshape(bs * ts, ssm_w), h_re, h_im

    def attend_s(q, kb, vb):
        o = sb_attention_sample(q.reshape(bs, ts, sb_w), kb.reshape(bs, ts, sb_w), vb.reshape(bs, ts, sb_w), bias,
                                cache_k[l].reshape(-1, PAGE_SIZE, sb_w), cache_v[l].reshape(-1, PAGE_SIZE, sb_w),
                                page_table)
        return o.reshape(bs * ts, sb_w).astype(BF16)

    y_s, k_s, v_s, hr_s, hi_s = mixer_and_moe(
        x_sample.reshape(bs * ts, d), attend_s,
        cache_mem_k[l].reshape(bs, n_mem, mem_w), cache_mem_v[l].reshape(bs, n_mem, mem_w), s5_s)

    n_heads = sb_w // SB_HEAD_DIM
    mem_dh = mem_w // MEM_HEADS
    return (y_p.reshape(bp, tp, d), y_s.reshape(bs, ts, d),
            k_p.reshape(1, bp, tp, n_heads, SB_HEAD_DIM), v_p.reshape(1, bp, tp, n_heads, SB_HEAD_DIM),
            k_s.reshape(1, bs, ts, n_heads, SB_HEAD_DIM), v_s.reshape(1, bs, ts, n_heads, SB_HEAD_DIM),
            hr_p.reshape(1, bp, n_groups, n_state), hi_p.reshape(1, bp, n_groups, n_state),
            hr_s.reshape(1, bs, n_groups, n_state), hi_s.reshape(1, bs, n_groups, n_state),
            mk_p.reshape(1, bp, n_mem, MEM_HEADS, mem_dh), mv_p.reshape(1, bp, n_mem, MEM_HEADS, mem_dh))
```

```python
import functools
import math

import jax
import jax.numpy as jnp
from jax import lax
from jax.experimental import pallas as pl
from jax.experimental.pallas import tpu as pltpu

F32 = jnp.float32
BF16 = jnp.bfloat16

V7X_VMEM_BYTES = 64 * 1024 * 1024
VMEM_LIMIT = V7X_VMEM_BYTES - 8 * 1024 * 1024
LANES = 128
SUBLANES = 8

SSM_GROUP = 16
SSM_STATE = 64
SB_HEAD_DIM = 128
MEM_HEADS = 4
PAGE_SIZE = 128
N_GROUPS = 4
EXPERTS_PER_GROUP = 8
N_EXPERTS = N_GROUPS * EXPERTS_PER_GROUP
RMS_EPS = 1e-6

SSM_BLOCK_GROUPS = 16
SSM_BLOCK_CH = SSM_BLOCK_GROUPS * SSM_GROUP
SSM_BLOCK_ST = SSM_BLOCK_GROUPS * SSM_STATE
SCAN_LANES = 512
ROUTER_EXPERT_LANE0 = 32


def _cparams(*sem):
    return pltpu.CompilerParams(dimension_semantics=sem, vmem_limit_bytes=VMEM_LIMIT)


def _sigmoid(x):
    return 1.0 / (1.0 + jnp.exp(-x))


def _tile(m, pref):
    t = min(m, pref)
    assert m % t == 0, (m, pref)
    return t


def _rms(x, g):
    return x * lax.rsqrt(jnp.mean(x * x, axis=-1, keepdims=True) + RMS_EPS) * g


def _rms_kernel(x_ref, g_ref, o_ref):
    o_ref[...] = _rms(x_ref[...], g_ref[...]).astype(o_ref.dtype)


def rmsnorm(x, g, out_dtype, tm=512):
    m, d = x.shape
    tm = _tile(m, tm)
    return pl.pallas_call(
        _rms_kernel,
        grid=(m // tm,),
        in_specs=[pl.BlockSpec((tm, d), lambda i: (i, 0)), pl.BlockSpec((1, d), lambda i: (0, 0))],
        out_specs=pl.BlockSpec((tm, d), lambda i: (i, 0)),
        out_shape=jax.ShapeDtypeStruct((m, d), out_dtype),
        compiler_params=_cparams("parallel"),
        name="rmsnorm",
    )(x, g.reshape(1, d))


def _mm_kernel(*refs, has_res):
    a_ref, b_ref = refs[0], refs[1]
    o_refs = refs[3:] if has_res else refs[2:]
    acc = jnp.dot(a_ref[...], b_ref[...], preferred_element_type=F32)
    if has_res:
        acc = refs[2][...] + acc
    for o_ref in o_refs:
        o_ref[...] = acc.astype(o_ref.dtype)


def matmul(a, b, out_dtypes, res=None, tm=512, tn=1024, name="matmul"):
    m, k = a.shape
    n = b.shape[1]
    tm, tn = _tile(m, tm), _tile(n, tn)
    in_specs = [pl.BlockSpec((tm, k), lambda i, j: (i, 0)), pl.BlockSpec((k, tn), lambda i, j: (0, j))]
    args = [a, b]
    if res is not None:
        in_specs.append(pl.BlockSpec((tm, tn), lambda i, j: (i, j)))
        args.append(res)
    outs = pl.pallas_call(
        functools.partial(_mm_kernel, has_res=res is not None),
        grid=(m // tm, n // tn),
        in_specs=in_specs,
        out_specs=[pl.BlockSpec((tm, tn), lambda i, j: (i, j)) for _ in out_dtypes],
        out_shape=[jax.ShapeDtypeStruct((m, n), dt) for dt in out_dtypes],
        compiler_params=_cparams("parallel", "arbitrary"),
        name=name,
    )(*args)
    return outs


def _proj_kernel(a_ref, b_ref, u_ref, q_ref, k_ref, kb_ref, v_ref, vb_ref, qm_ref, *, q_scale):
    j = pl.program_id(1)
    acc = jnp.dot(a_ref[...], b_ref[...], preferred_element_type=F32)

    @pl.when(j == 0)
    def _():
        u_ref[...] = acc

    @pl.when(j == 1)
    def _():
        q_ref[...] = (acc * q_scale).astype(BF16)

    @pl.when(j == 2)
    def _():
        k_ref[...] = acc
        kb_ref[...] = acc.astype(BF16)

    @pl.when(j == 3)
    def _():
        v_ref[...] = acc
        vb_ref[...] = acc.astype(BF16)

    @pl.when(j == 4)
    def _():
        qm_ref[...] = acc.astype(BF16)


def in_projection(hn, w5, width, tm=512):
    m, k = hn.shape
    tm = _tile(m, tm)
    dts = (F32, BF16, F32, BF16, F32, BF16, BF16)
    return pl.pallas_call(
        functools.partial(_proj_kernel, q_scale=-(SB_HEAD_DIM ** -0.5)),
        grid=(m // tm, 5),
        in_specs=[pl.BlockSpec((tm, k), lambda i, j: (i, 0)), pl.BlockSpec((k, width), lambda i, j: (0, j))],
        out_specs=[pl.BlockSpec((tm, width), lambda i, j: (i, 0)) for _ in dts],
        out_shape=[jax.ShapeDtypeStruct((m, width), dt) for dt in dts],
        compiler_params=_cparams("parallel", "arbitrary"),
        name="in_projection",
    )(hn, w5)


def _cmul_add(ar, ai, hr, hi, xr, xi):
    return ar * hr - ai * hi + xr, ar * hi + ai * hr + xi


def _s5_readout(q, x_ref, u_ref, cblk_ref, d_ref, z_ref):
    cols = slice(q * SSM_BLOCK_CH, (q + 1) * SSM_BLOCK_CH)
    y = jnp.dot(x_ref[...].astype(BF16), cblk_ref[q], preferred_element_type=F32)
    y = y + d_ref[:, cols] * u_ref[:, cols]
    z_ref[:, cols] = jax.nn.gelu(y)


def _s5_glu(z_ref, wglu_ref, bglu_ref):
    z = z_ref[...]
    gate = _sigmoid(jnp.dot(z.astype(BF16), wglu_ref[...], preferred_element_type=F32) + bglu_ref[...])
    return z * gate


def _s5_prompt_kernel(u_ref, h0r_ref, h0i_ref, bblk_ref, cblk_ref, d_ref, apr_ref, api_ref, wglu_ref, bglu_ref,
                      o_ref, hr_out_ref, hi_out_ref, x_ref, z_ref, hcr_ref, hci_ref, *, nj):
    n_blocks = bblk_ref.shape[0]

    @pl.when(pl.program_id(0) == 0)
    def _():
        hcr_ref[...] = h0r_ref[...]
        hci_ref[...] = h0i_ref[...]

    for q in range(n_blocks):
        ub = u_ref[:, q * SSM_BLOCK_CH:(q + 1) * SSM_BLOCK_CH]
        x_ref[...] = jnp.dot(ub.astype(BF16), bblk_ref[q], preferred_element_type=F32)
        for c in range(SSM_BLOCK_ST // SCAN_LANES):
            re = slice(c * SCAN_LANES, (c + 1) * SCAN_LANES)
            im = slice(SSM_BLOCK_ST + c * SCAN_LANES, SSM_BLOCK_ST + (c + 1) * SCAN_LANES)
            st = slice(q * SSM_BLOCK_ST + c * SCAN_LANES, q * SSM_BLOCK_ST + (c + 1) * SCAN_LANES)
            ar = jnp.broadcast_to(apr_ref[0:1, st], (SUBLANES, SCAN_LANES))
            ai = jnp.broadcast_to(api_ref[0:1, st], (SUBLANES, SCAN_LANES))

            def scan_body(j, carry, re=re, im=im, ar=ar, ai=ai):
                rows = pl.ds(pl.multiple_of(j * SUBLANES, SUBLANES), SUBLANES)
                hr, hi = _cmul_add(ar, ai, carry[0], carry[1], x_ref[rows, re], x_ref[rows, im])
                x_ref[rows, re] = hr
                x_ref[rows, im] = hi
                return hr, hi

            zero = jnp.zeros((SUBLANES, SCAN_LANES), F32)
            er, ei = lax.fori_loop(0, nj, scan_body, (zero, zero), unroll=2)

            alr, ali = apr_ref[nj - 1:nj, st], api_ref[nj - 1:nj, st]
            cr, ci = hcr_ref[:, st], hci_ref[:, st]
            crs, cis = [], []
            for s in range(SUBLANES):
                crs.append(cr)
                cis.append(ci)
                cr, ci = _cmul_add(alr, ali, cr, ci, er[s:s + 1, :], ei[s:s + 1, :])
            hcr_ref[:, st] = cr
            hci_ref[:, st] = ci
            cin_r = jnp.concatenate(crs, axis=0)
            cin_i = jnp.concatenate(cis, axis=0)

            def fix_body(j, _, re=re, im=im, st=st, cin_r=cin_r, cin_i=cin_i):
                rows = pl.ds(pl.multiple_of(j * SUBLANES, SUBLANES), SUBLANES)
                pr = jnp.broadcast_to(apr_ref[pl.ds(j, 1), st], (SUBLANES, SCAN_LANES))
                pi = jnp.broadcast_to(api_ref[pl.ds(j, 1), st], (SUBLANES, SCAN_LANES))
                hr, hi = _cmul_add(pr, pi, cin_r, cin_i, x_ref[rows, re], x_ref[rows, im])
                x_ref[rows, re] = hr
                x_ref[rows, im] = hi
                return 0

            lax.fori_loop(0, nj, fix_body, 0, unroll=2)
        _s5_readout(q, x_ref, u_ref, cblk_ref, d_ref, z_ref)

    o_ref[...] = _s5_glu(z_ref, wglu_ref, bglu_ref).astype(o_ref.dtype)
    hr_out_ref[...] = hcr_ref[...]
    hi_out_ref[...] = hci_ref[...]


def _s5_sample_kernel(u_ref, h0r_ref, h0i_ref, bblk_ref, cblk_ref, d_ref, apr_ref, api_ref, wglu_ref, bglu_ref,
                      o_ref, hr_out_ref, hi_out_ref, x_ref, z_ref, *, nb, nt):
    n_blocks = bblk_ref.shape[0]
    for q in range(n_blocks):
        ub = u_ref[:, q * SSM_BLOCK_CH:(q + 1) * SSM_BLOCK_CH]
        x_ref[...] = jnp.dot(ub.astype(BF16), bblk_ref[q], preferred_element_type=F32)
        for c in range(SSM_BLOCK_ST // SCAN_LANES):
            re = slice(c * SCAN_LANES, (c + 1) * SCAN_LANES)
            im = slice(SSM_BLOCK_ST + c * SCAN_LANES, SSM_BLOCK_ST + (c + 1) * SCAN_LANES)
            st = slice(q * SSM_BLOCK_ST + c * SCAN_LANES, q * SSM_BLOCK_ST + (c + 1) * SCAN_LANES)
            ar, ai = apr_ref[0:1, st], api_ref[0:1, st]
            hr, hi = h0r_ref[:, st], h0i_ref[:, st]
            for t in range(nt):
                rows = slice(t * nb, (t + 1) * nb)
                hr, hi = _cmul_add(ar, ai, hr, hi, x_ref[rows, re], x_ref[rows, im])
                x_ref[rows, re] = hr
                x_ref[rows, im] = hi
            hr_out_ref[:, st] = hr
            hi_out_ref[:, st] = hi
        _s5_readout(q, x_ref, u_ref, cblk_ref, d_ref, z_ref)
    o_ref[...] = _s5_glu(z_ref, wglu_ref, bglu_ref).astype(o_ref.dtype)


def _s5_params(lam_re, lam_im, log_step, b_re, b_im, c_re, c_im, d_skip, n_pow):
    g, p = lam_re.shape
    dt = jnp.exp(log_step)[:, None]
    steps = jnp.arange(1, n_pow + 1, dtype=F32)[:, None, None]
    mag = jnp.exp(lam_re * dt * steps)
    phase = lam_im * dt * steps
    ap_re = (mag * jnp.cos(phase)).reshape(n_pow, g * p)
    ap_im = (mag * jnp.sin(phase)).reshape(n_pow, g * p)
    a_re, a_im = mag[0] * jnp.cos(phase[0]), mag[0] * jnp.sin(phase[0])
    num_re = a_re - 1.0
    den = lam_re * lam_re + lam_im * lam_im
    k_re = (num_re * lam_re + a_im * lam_im) / den
    k_im = (a_im * lam_re - num_re * lam_im) / den
    bp_re = k_re[..., None] * b_re - k_im[..., None] * b_im
    bp_im = k_re[..., None] * b_im + k_im[..., None] * b_re
    nblk = g // SSM_BLOCK_GROUPS
    eye = jnp.eye(SSM_BLOCK_GROUPS, dtype=F32)

    def b_block(bp):
        bp = bp.reshape(nblk, SSM_BLOCK_GROUPS, p, SSM_GROUP)
        return jnp.einsum('qgpc,gh->qgchp', bp, eye).reshape(nblk, SSM_BLOCK_CH, SSM_BLOCK_ST)

    def c_block(c):
        c = c.reshape(nblk, SSM_BLOCK_GROUPS, SSM_GROUP, p)
        return jnp.einsum('qgcp,gh->qgphc', c, eye).reshape(nblk, SSM_BLOCK_ST, SSM_BLOCK_CH)

    bblk = jnp.concatenate([b_block(bp_re), b_block(bp_im)], axis=2).astype(BF16)
    cblk = jnp.concatenate([c_block(c_re), -c_block(c_im)], axis=1).astype(BF16)
    return bblk, cblk, d_skip.reshape(1, g * SSM_GROUP), ap_re, ap_im


def _const_spec(shape):
    return pl.BlockSpec(shape, lambda *_: (0,) * len(shape))


def s5_prompt(u, h0r, h0i, params, w_glu, b_glu, chunk):
    t, w = u.shape
    nj = chunk // SUBLANES
    bblk, cblk, d, apr, api = params
    ns = apr.shape[1]
    consts = [h0r, h0i, bblk, cblk, d, apr, api, w_glu, b_glu.reshape(1, w)]
    u_perm = u.reshape(t // chunk, SUBLANES, nj, w).transpose(0, 2, 1, 3).reshape(t, w)
    o_perm, h_re, h_im = pl.pallas_call(
        functools.partial(_s5_prompt_kernel, nj=nj),
        grid=(t // chunk,),
        in_specs=[pl.BlockSpec((chunk, w), lambda i: (i, 0))] + [_const_spec(c.shape) for c in consts],
        out_specs=[pl.BlockSpec((chunk, w), lambda i: (i, 0)), _const_spec((1, ns)), _const_spec((1, ns))],
        out_shape=[jax.ShapeDtypeStruct((t, w), BF16), jax.ShapeDtypeStruct((1, ns), F32),
                   jax.ShapeDtypeStruct((1, ns), F32)],
        scratch_shapes=[pltpu.VMEM((chunk, 2 * SSM_BLOCK_ST), F32), pltpu.VMEM((chunk, w), F32),
                        pltpu.VMEM((1, ns), F32), pltpu.VMEM((1, ns), F32)],
        compiler_params=_cparams("arbitrary"),
        name="s5_prompt",
    )(u_perm, *consts)
    o = o_perm.reshape(t // chunk, nj, SUBLANES, w).transpose(0, 2, 1, 3).reshape(t, w)
    return o, h_re, h_im


def s5_sample(u_tb, h0r, h0i, params, w_glu, b_glu, nb, nt):
    m, w = u_tb.shape
    bblk, cblk, d, apr, api = params
    ns = apr.shape[1]
    consts = [h0r, h0i, bblk, cblk, d, apr, api, w_glu, b_glu.reshape(1, w)]
    return pl.pallas_call(
        functools.partial(_s5_sample_kernel, nb=nb, nt=nt),
        grid=(1,),
        in_specs=[_const_spec(u_tb.shape)] + [_const_spec(c.shape) for c in consts],
        out_specs=[_const_spec((m, w)), _const_spec((nb, ns)), _const_spec((nb, ns))],
        out_shape=[jax.ShapeDtypeStruct((m, w), BF16), jax.ShapeDtypeStruct((nb, ns), F32),
                   jax.ShapeDtypeStruct((nb, ns), F32)],
        scratch_shapes=[pltpu.VMEM((m, 2 * SSM_BLOCK_ST), F32), pltpu.VMEM((m, w), F32)],
        compiler_params=_cparams("arbitrary"),
        name="s5_sample",
    )(u_tb, *consts)


def _log_sigmoid_pair(w):
    lk = jnp.minimum(w, 0.0) - jnp.log(1.0 + jnp.exp(-jnp.abs(w)))
    return lk - w, lk


def _sb_prompt_kernel(bias_ref, q_ref, k_ref, v_ref, o_ref, acc_ref, carry_ref, *, tq, tk, n_hp):
    hp, i = pl.program_id(0), pl.program_id(1)
    kb_per_q = tq // tk
    row = lax.broadcasted_iota(jnp.int32, (tq, tk), 0)
    col = lax.broadcasted_iota(jnp.int32, (tq, tk), 1)
    newer = (lax.broadcasted_iota(jnp.int32, (tk, tk), 0)
             > lax.broadcasted_iota(jnp.int32, (tk, tk), 1)).astype(BF16)
    acc_ref[...] = jnp.zeros_like(acc_ref)
    carry_ref[...] = jnp.zeros_like(carry_ref)

    def tile(kb, mask):
        rows = pl.ds(pl.multiple_of(kb * tk, tk), tk)
        for hh in range(n_hp):
            cols = slice(hh * SB_HEAD_DIM, (hh + 1) * SB_HEAD_DIM)
            w = lax.dot_general(q_ref[:, cols], k_ref[rows, cols], (((1,), (1,)), ((), ())),
                                preferred_element_type=F32) - bias_ref[hp * n_hp + hh]
            ls, lk = _log_sigmoid_pair(w)
            if mask is not None:
                lk = jnp.where(mask, lk, 0.0)
            after = jnp.dot(lk.astype(BF16), newer, preferred_element_type=F32) + carry_ref[hh]
            a = jnp.exp(ls + after)
            if mask is not None:
                a = jnp.where(mask, a, 0.0)
            acc_ref[hh] += jnp.dot(a.astype(BF16), v_ref[rows, cols], preferred_element_type=F32)
            carry_ref[hh] += jnp.sum(lk, axis=1, keepdims=True)

    for d in reversed(range(kb_per_q)):
        tile(i * kb_per_q + d, col + d * tk < row)

    def body(jj, _):
        tile(i * kb_per_q - 1 - jj, None)
        return 0

    lax.fori_loop(0, i * kb_per_q, body, 0)
    for hh in range(n_hp):
        o_ref[:, hh * SB_HEAD_DIM:(hh + 1) * SB_HEAD_DIM] = acc_ref[hh].astype(o_ref.dtype)


def sb_attention_prompt(q, k, v, bias, tq=512, tk=256, heads_per_step=2):
    t, w = q.shape
    n_heads = w // SB_HEAD_DIM
    tq, tk, n_hp = _tile(t, tq), _tile(t, tk), heads_per_step
    assert tq % tk == 0 and n_heads % n_hp == 0
    wb = n_hp * SB_HEAD_DIM
    grid_spec = pltpu.PrefetchScalarGridSpec(
        num_scalar_prefetch=1,
        grid=(n_heads // n_hp, t // tq),
        in_specs=[pl.BlockSpec((tq, wb), lambda h, i, b: (i, h)),
                  pl.BlockSpec((t, wb), lambda h, i, b: (0, h)),
                  pl.BlockSpec((t, wb), lambda h, i, b: (0, h))],
        out_specs=pl.BlockSpec((tq, wb), lambda h, i, b: (i, h)),
        scratch_shapes=[pltpu.VMEM((n_hp, tq, SB_HEAD_DIM), F32), pltpu.VMEM((n_hp, tq, 1), F32)],
    )
    return pl.pallas_call(
        functools.partial(_sb_prompt_kernel, tq=tq, tk=tk, n_hp=n_hp),
        grid_spec=grid_spec,
        out_shape=jax.ShapeDtypeStruct((t, w), BF16),
        compiler_params=_cparams("parallel", "arbitrary"),
        name="sb_attention_prompt",
    )(bias, q, k, v)


def _sb_sample_kernel(pt_ref, *refs, n_pp, n_heads):
    wq_ref, bias_ref, kn_ref, vn_ref = refs[:4]
    kp_refs = refs[4:4 + n_pp]
    vp_refs = refs[4 + n_pp:4 + 2 * n_pp]
    o_ref, acc_ref, carry_ref = refs[4 + 2 * n_pp:]
    p = pl.program_id(1)
    tk = PAGE_SIZE
    nq = o_ref.shape[0]
    dh = SB_HEAD_DIM
    newer = (lax.broadcasted_iota(jnp.int32, (tk, tk), 1)
             > lax.broadcasted_iota(jnp.int32, (tk, tk), 0)).astype(BF16)

    def head_rows(ref, h):
        return ref[pl.ds(h, tk, stride=n_heads), :]

    def weights(z, mask, carry):
        ls, lk = _log_sigmoid_pair(z)
        if mask is not None:
            lk = jnp.where(mask, lk, 0.0)
        after = jnp.dot(newer, lk.astype(BF16), preferred_element_type=F32) + carry
        a = jnp.exp(ls + after)
        if mask is not None:
            a = jnp.where(mask, a, 0.0)
        return a.T, jnp.sum(lk, axis=0, keepdims=True)

    @pl.when(p == 0)
    def _():
        lane_q = lax.broadcasted_iota(jnp.int32, (tk, LANES), 1) % nq
        key = lax.broadcasted_iota(jnp.int32, (tk, LANES), 0)
        z = jnp.dot(kn_ref[...], wq_ref[...], preferred_element_type=F32) - bias_ref[...]
        at, tot = weights(z, key < lane_q, jnp.zeros((1, LANES), F32))
        carry_ref[...] = tot
        for h in range(n_heads):
            acc_ref[h * nq:(h + 1) * nq, :] = jnp.dot(at[h * nq:(h + 1) * nq, :].astype(BF16),
                                                      vn_ref[:, h * dh:(h + 1) * dh], preferred_element_type=F32)

    z = -bias_ref[...]
    for hp in range(n_heads // 2):
        lhs = jnp.concatenate(
            [jnp.concatenate([head_rows(kp_refs[r], 2 * hp), head_rows(kp_refs[r], 2 * hp + 1)], axis=1)
             for r in range(n_pp)], axis=0).astype(BF16)
        z = z + jnp.dot(lhs, wq_ref[2 * hp * dh:(2 * hp + 2) * dh, :], preferred_element_type=F32)
    carry = carry_ref[...]
    ats = []
    for r in range(n_pp):
        at, tot = weights(z[r * tk:(r + 1) * tk, :], None, carry)
        carry = carry + tot
        ats.append(at)
    carry_ref[...] = carry
    for h in range(n_heads):
        lhs = jnp.concatenate([at[h * nq:(h + 1) * nq, :] for at in ats], axis=1).astype(BF16)
        rhs = jnp.concatenate([head_rows(vp_refs[r], h) for r in range(n_pp)], axis=0).astype(BF16)
        acc_ref[h * nq:(h + 1) * nq, :] += jnp.dot(lhs, rhs, preferred_element_type=F32)

    @pl.when(p == pl.num_programs(1) - 1)
    def _():
        for h in range(n_heads):
            o_ref[:, h * dh:(h + 1) * dh] = acc_ref[h * nq:(h + 1) * nq, :]


def sb_attention_sample(q, k_new, v_new, bias, cache_k, cache_v, page_table, pages_per_step=8):
    nb, nq, w = q.shape
    n_heads = w // SB_HEAD_DIM
    n_pages = page_table.shape[1]
    n_pp = pages_per_step
    assert n_pages % n_pp == 0 and n_heads * nq <= LANES and nq <= PAGE_SIZE and n_heads % 2 == 0
    q4 = q.reshape(nb, nq, n_heads, SB_HEAD_DIM).transpose(0, 2, 3, 1)
    wq = jnp.einsum('bhdt,hg->bhdgt', q4, jnp.eye(n_heads, dtype=q.dtype)).reshape(nb, w, n_heads * nq)
    wq = jnp.pad(wq, ((0, 0), (0, 0), (0, LANES - n_heads * nq)))
    bias_row = jnp.pad(jnp.repeat(bias, nq), (0, LANES - n_heads * nq)).reshape(1, LANES)
    pad = ((0, 0), (0, PAGE_SIZE - nq), (0, 0))
    k_new, v_new = jnp.pad(k_new, pad), jnp.pad(v_new, pad)

    page_rows = PAGE_SIZE * n_heads

    def page_spec(r):
        return pl.BlockSpec((None, page_rows, SB_HEAD_DIM),
                            lambda b, p, pt: (pt[b, n_pages - 1 - (p * n_pp + r)], 0, 0))

    per_b = lambda b, p, pt: (b, 0, 0)
    grid_spec = pltpu.PrefetchScalarGridSpec(
        num_scalar_prefetch=1,
        grid=(nb, n_pages // n_pp),
        in_specs=[pl.BlockSpec((None, w, LANES), per_b), pl.BlockSpec((1, LANES), lambda b, p, pt: (0, 0)),
                  pl.BlockSpec((None, PAGE_SIZE, w), per_b), pl.BlockSpec((None, PAGE_SIZE, w), per_b)]
                 + [page_spec(r) for r in range(n_pp)] + [page_spec(r) for r in range(n_pp)],
        out_specs=pl.BlockSpec((None, nq, w), per_b),
        scratch_shapes=[pltpu.VMEM((n_heads * nq, SB_HEAD_DIM), F32), pltpu.VMEM((1, LANES), F32)],
    )
    return pl.pallas_call(
        functools.partial(_sb_sample_kernel, n_pp=n_pp, n_heads=n_heads),
        grid_spec=grid_spec,
        out_shape=jax.ShapeDtypeStruct((nb, nq, w), F32),
        compiler_params=_cparams("parallel", "arbitrary"),
        name="sb_attention_sample",
    )(page_table, wq, bias_row, k_new, v_new, *([cache_k] * n_pp), *([cache_v] * n_pp))


def _mem_attn_kernel(q_ref, k_ref, v_ref, o_ref, *, scale):
    dh = o_ref.shape[1] // MEM_HEADS
    for h in range(MEM_HEADS):
        cols = slice(h * dh, (h + 1) * dh)
        kh = k_ref[:, cols].astype(BF16)
        vh = v_ref[:, cols].astype(BF16)
        s = lax.dot_general(q_ref[:, cols], kh, (((1,), (1,)), ((), ())), preferred_element_type=F32) * scale
        e = jnp.exp(s - jnp.max(s, axis=-1, keepdims=True))
        p = e / jnp.sum(e, axis=-1, keepdims=True)
        o_ref[:, cols] = jnp.dot(p.astype(BF16), vh, preferred_element_type=F32).astype(o_ref.dtype)


def memory_attention(q, mem_k, mem_v, tq=512):
    nb, t, w = q.shape
    n_mem = mem_k.shape[1]
    tq = _tile(t, tq)
    return pl.pallas_call(
        functools.partial(_mem_attn_kernel, scale=(w // MEM_HEADS) ** -0.5),
        grid=(nb, t // tq),
        in_specs=[pl.BlockSpec((None, tq, w), lambda b, i: (b, i, 0)),
                  pl.BlockSpec((None, n_mem, w), lambda b, i: (b, 0, 0)),
                  pl.BlockSpec((None, n_mem, w), lambda b, i: (b, 0, 0))],
        out_specs=pl.BlockSpec((None, tq, w), lambda b, i: (b, i, 0)),
        out_shape=jax.ShapeDtypeStruct((nb, t, w), BF16),
        compiler_params=_cparams("parallel", "arbitrary"),
        name="memory_attention",
    )(q, mem_k, mem_v)


def _merge_kernel(hn_ref, o0_ref, o1_ref, o2_ref, g0_ref, g1_ref, g2_ref, b0_ref, b1_ref, b2_ref, out_ref):
    hn = hn_ref[...]
    total = None
    for o_ref, g_ref, b_ref in ((o0_ref, g0_ref, b0_ref), (o1_ref, g1_ref, b1_ref), (o2_ref, g2_ref, b2_ref)):
        gate = _sigmoid(jnp.dot(hn, g_ref[...], preferred_element_type=F32))
        term = gate * jnp.dot(o_ref[...], b_ref[...], preferred_element_type=F32)
        total = term if total is None else total + term
    out_ref[...] = total.astype(out_ref.dtype)


def gated_merge(hn, branches, w_gates, w_branches, tm=512, tn=512):
    m, d = hn.shape
    tm, tn = _tile(m, tm), _tile(d, tn)
    nj = d // tn
    row = lambda i, j: (i, 0)
    in_specs = [pl.BlockSpec((tm, d), row)]
    in_specs += [pl.BlockSpec((tm, o.shape[1]), row) for o in branches]
    in_specs += [pl.BlockSpec((d, tn), functools.partial(lambda i, j, b: (0, b * nj + j), b=b)) for b in range(3)]
    in_specs += [pl.BlockSpec((w.shape[0], tn), lambda i, j: (0, j)) for w in w_branches]
    return pl.pallas_call(
        _merge_kernel,
        grid=(m // tm, nj),
        in_specs=in_specs,
        out_specs=pl.BlockSpec((tm, tn), lambda i, j: (i, j)),
        out_shape=jax.ShapeDtypeStruct((m, d), BF16),
        compiler_params=_cparams("parallel", "arbitrary"),
        name="gated_merge",
    )(hn, *branches, w_gates, w_gates, w_gates, *w_branches)


def _router_kernel(x_ref, g_ref, wr_ref, br_ref, hn_ref, gate_ref):
    hn = _rms(x_ref[...], g_ref[...])
    hn_ref[...] = hn.astype(hn_ref.dtype)
    logits = jnp.dot(hn, wr_ref[...], preferred_element_type=F32, precision=lax.Precision.HIGHEST) + br_ref[...]
    lane = lax.broadcasted_iota(jnp.int32, logits.shape, 1).astype(F32)
    neg, far = -1e30, 1e9
    rmax = lambda v: jnp.max(v, axis=-1, keepdims=True)
    rmin = lambda v: jnp.min(v, axis=-1, keepdims=True)
    rsum = lambda v: jnp.sum(v, axis=-1, keepdims=True)
    is_group = lane < N_GROUPS
    gl = jnp.where(is_group, logits, neg)
    gmax = rmax(gl)
    g_val = 1.0 / rsum(jnp.where(is_group, jnp.exp(gl - gmax), 0.0))
    g_idx = rmin(jnp.where(gl == gmax, lane, far))
    lo = ROUTER_EXPERT_LANE0 + EXPERTS_PER_GROUP * g_idx
    in_group = jnp.logical_and(lane >= lo, lane < lo + EXPERTS_PER_GROUP)
    el = jnp.where(in_group, logits, neg)
    ex = jnp.where(in_group, jnp.exp(el - rmax(el)), 0.0)
    prob = jnp.where(in_group, ex / rsum(ex), -1.0)
    v1 = rmax(prob)
    i1 = rmin(jnp.where(prob == v1, lane, far))
    rest = jnp.where(lane == i1, -1.0, prob)
    v2 = rmax(rest)
    i2 = rmin(jnp.where(rest == v2, lane, far))
    within = jnp.where(lane == i1, v1 / (v1 + v2), jnp.where(lane == i2, v2 / (v1 + v2), 0.0))
    gate_ref[...] = g_val * within


def moe_router(x, g, w_router, b_router, tm=256):
    m, d = x.shape
    tm = _tile(m, tm)
    return pl.pallas_call(
        _router_kernel,
        grid=(m // tm,),
        in_specs=[pl.BlockSpec((tm, d), lambda i: (i, 0)), _const_spec((1, d)), _const_spec((d, LANES)),
                  _const_spec((1, LANES))],
        out_specs=[pl.BlockSpec((tm, d), lambda i: (i, 0)), pl.BlockSpec((tm, LANES), lambda i: (i, 0))],
        out_shape=[jax.ShapeDtypeStruct((m, d), BF16), jax.ShapeDtypeStruct((m, LANES), F32)],
        compiler_params=_cparams("parallel"),
        name="moe_router",
    )(x, g.reshape(1, d), w_router, b_router)


def _experts_kernel(hn_ref, gate_ref, x_ref, wgu_ref, wd_ref, gf_ref, o_ref):
    e = pl.program_id(1)
    ff = wd_ref.shape[0]

    @pl.when(e == 0)
    def _():
        o_ref[...] = x_ref[...]

    lane = lax.broadcasted_iota(jnp.int32, gate_ref.shape, 1)
    gate = jnp.sum(jnp.where(lane == e + ROUTER_EXPERT_LANE0, gate_ref[...], 0.0), axis=-1, keepdims=True)
    au = jnp.dot(hn_ref[...], wgu_ref[...], preferred_element_type=F32)
    a, up = au[:, :ff], au[:, ff:]
    act = a * _sigmoid(a) * up * gate
    o_ref[...] += jnp.dot(act.astype(BF16), wd_ref[...], preferred_element_type=F32)

    @pl.when(e == pl.num_programs(1) - 1)
    def _():
        o_ref[...] = _rms(o_ref[...], gf_ref[...])


def moe_experts_final_norm(hn, gate, x, w_gate_up, w_down, g_final, tm=1024):
    m, d = x.shape
    n_exp, _, ff2 = w_gate_up.shape
    tm = _tile(m, tm)
    row = lambda i, e: (i, 0)
    return pl.pallas_call(
        _experts_kernel,
        grid=(m // tm, n_exp),
        in_specs=[pl.BlockSpec((tm, d), row), pl.BlockSpec((tm, LANES), row), pl.BlockSpec((tm, d), row),
                  pl.BlockSpec((None, d, ff2), lambda i, e: (e, 0, 0)),
                  pl.BlockSpec((None, ff2 // 2, d), lambda i, e: (e, 0, 0)),
                  pl.BlockSpec((1, d), lambda i, e: (0, 0))],
        out_specs=pl.BlockSpec((tm, d), row),
        out_shape=jax.ShapeDtypeStruct((m, d), F32),
        compiler_params=_cparams("parallel", "arbitrary"),
        name="moe_experts",
    )(hn, gate, x, w_gate_up, w_down, g_final.reshape(1, d))


S5_CHUNK = 512


def kernel(x_prompt, x_sample, mem_prompt, cache_k, cache_v, page_table, cache_mem_k, cache_mem_v, state_ssm_re, state_ssm_im, g_mix_norm, w_in, sb_logit_bias, ssm_lambda_re, ssm_lambda_im, ssm_log_step, ssm_b_re, ssm_b_im, ssm_c_re, ssm_c_im, ssm_d, w_glu, b_glu, g_mem_norm, w_mem_k, w_mem_v, w_branch_ssm, w_branch_sb, w_branch_mem, w_out, g_ffn_norm, w_group_router, b_group_router, w_expert_router, b_expert_router, w_exp_gate, w_exp_up, w_exp_down, g_final):
    depth = w_in.shape[0]
    assert depth == 1, "one mixer + MoE layer per step"
    bp, tp, d = x_prompt.shape
    bs, ts, _ = x_sample.shape
    assert bp == 1
    n_groups, n_state = ssm_lambda_re.shape[1:]
    ssm_w = n_groups * SSM_GROUP
    n_st = n_groups * n_state
    sb_w = cache_k.shape[3] * cache_k.shape[4]
    mem_w = cache_mem_k.shape[3] * cache_mem_k.shape[4]
    n_mem = mem_prompt.shape[1]
    assert ssm_w == sb_w == mem_w
    n_heads = sb_w // SB_HEAD_DIM
    mem_dh = mem_w // MEM_HEADS
    l = 0

    w5 = w_in[l, :, :5 * ssm_w].astype(BF16)
    w_gates = w_in[l, :, 5 * ssm_w:].astype(BF16)
    w_branches = [w_branch_ssm[l].astype(BF16), w_branch_sb[l].astype(BF16), w_branch_mem[l].astype(BF16)]
    w_out_b = w_out[l].astype(BF16)
    w_glu_b = w_glu[l].astype(BF16)
    w_memkv = jnp.concatenate([w_mem_k[l], w_mem_v[l]], axis=1).astype(BF16)
    w_gate_up = jnp.concatenate([w_exp_gate[l], w_exp_up[l]], axis=2).astype(BF16)
    w_down = w_exp_down[l].astype(BF16)
    w_router = jnp.zeros((d, LANES), F32)
    w_router = w_router.at[:, :N_GROUPS].set(w_group_router[l])
    w_router = w_router.at[:, ROUTER_EXPERT_LANE0:ROUTER_EXPERT_LANE0 + N_EXPERTS].set(w_expert_router[l])
    b_router = jnp.zeros((1, LANES), F32)
    b_router = b_router.at[0, :N_GROUPS].set(b_group_router[l])
    b_router = b_router.at[0, ROUTER_EXPERT_LANE0:ROUTER_EXPERT_LANE0 + N_EXPERTS].set(b_expert_router[l])
    s5p = _s5_params(ssm_lambda_re[l], ssm_lambda_im[l], ssm_log_step[l], ssm_b_re[l], ssm_b_im[l],
                     ssm_c_re[l], ssm_c_im[l], ssm_d[l], S5_CHUNK // SUBLANES)
    bias = sb_logit_bias[l]

    mn = rmsnorm(mem_prompt.reshape(n_mem, d), g_mem_norm[l], BF16)
    (mem_kv,) = matmul(mn, w_memkv, (F32,), tm=n_mem, tn=1024, name="memory_kv")
    mk_p, mv_p = mem_kv[:, :mem_w], mem_kv[:, mem_w:]

    def mixer_and_moe(x, attend, mem_k, mem_v, s5):
        hn = rmsnorm(x, g_mix_norm[l], BF16)
        u, q, k, kb, v, vb, qm = in_projection(hn, w5, ssm_w)
        o_ssm, h_re, h_im = s5(u)
        o_sb = attend(q, kb, vb)
        o_mem = memory_attention(qm.reshape(mem_k.shape[0], -1, mem_w), mem_k, mem_v).reshape(-1, mem_w)
        merged = gated_merge(hn, (o_ssm, o_sb, o_mem), w_gates, w_branches)
        (x2,) = matmul(merged, w_out_b, (F32,), res=x, tm=512, tn=1024, name="out_projection")
        hn2, gate = moe_router(x2, g_ffn_norm[l], w_router, b_router)
        y = moe_experts_final_norm(hn2, gate, x2, w_gate_up, w_down, g_final)
        return y, k, v, h_re, h_im

    zeros = jnp.zeros((1, n_st), F32)
    y_p, k_p, v_p, hr_p, hi_p = mixer_and_moe(
        x_prompt.reshape(tp, d),
        lambda q, kb, vb: sb_attention_prompt(q, kb, vb, bias),
        mk_p.reshape(1, n_mem, mem_w), mv_p.reshape(1, n_mem, mem_w),
        lambda u: s5_prompt(u, zeros, zeros, s5p, w_glu_b, b_glu[l], S5_CHUNK))

    def s5_s(u):
        u_tb = u.reshape(bs, ts, ssm_w).transpose(1, 0, 2).reshape(ts * bs, ssm_w)
        o_tb, h_re, h_im = s5_sample(u_tb, state_ssm_re[l].reshape(bs, n_st), state_ssm_im[l].reshape(bs, n_st),
                                     s5p, w_glu_b, b_glu[l], bs, ts)
        return o_tb.reshape(ts, bs, ssm_w).transpose(1, 0, 2).reshape(bs * ts, ssm_w), h_re, h_im

    def attend_s(q, kb, vb):
        o = sb_attention_sample(q.reshape(bs, ts, sb_w), kb.reshape(bs, ts, sb_w), vb.reshape(bs, ts, sb_w), bias,
                                cache_k.reshape(-1, PAGE_SIZE * n_heads, SB_HEAD_DIM),
                                cache_v.reshape(-1, PAGE_SIZE * n_heads, SB_HEAD_DIM),
                                page_table + l * cache_k.shape[1])
        return o.reshape(bs * ts, sb_w).astype(BF16)

    y_s, k_s, v_s, hr_s, hi_s = mixer_and_moe(
        x_sample.reshape(bs * ts, d), attend_s,
        cache_mem_k[l].reshape(bs, n_mem, mem_w), cache_mem_v[l].reshape(bs, n_mem, mem_w), s5_s)

    return (y_p.reshape(bp, tp, d), y_s.reshape(bs, ts, d),
            k_p.reshape(1, bp, tp, n_heads, SB_HEAD_DIM), v_p.reshape(1, bp, tp, n_heads, SB_HEAD_DIM),
            k_s.reshape(1, bs, ts, n_heads, SB_HEAD_DIM), v_s.reshape(1, bs, ts, n_heads, SB_HEAD_DIM),
            hr_p.reshape(1, bp, n_groups, n_state), hi_p.reshape(1, bp, n_groups, n_state),
            hr_s.reshape(1, bs, n_groups, n_state), hi_s.reshape(1, bs, n_groups, n_state),
            mk_p.reshape(1, bp, n_mem, MEM_HEADS, mem_dh), mv_p.reshape(1, bp, n_mem, MEM_HEADS, mem_dh))
```

```python
import functools
import math

import jax
import jax.numpy as jnp
from jax import lax
from jax.experimental import pallas as pl
from jax.experimental.pallas import tpu as pltpu

F32 = jnp.float32
BF16 = jnp.bfloat16

V7X_VMEM_BYTES = 64 * 1024 * 1024
VMEM_LIMIT = V7X_VMEM_BYTES - 8 * 1024 * 1024
LANES = 128
SUBLANES = 8

SSM_GROUP = 16
SSM_STATE = 64
SB_HEAD_DIM = 128
MEM_HEADS = 4
PAGE_SIZE = 128
N_GROUPS = 4
EXPERTS_PER_GROUP = 8
N_EXPERTS = N_GROUPS * EXPERTS_PER_GROUP
RMS_EPS = 1e-6

SSM_BLOCK_GROUPS = 16
SSM_BLOCK_CH = SSM_BLOCK_GROUPS * SSM_GROUP
SSM_BLOCK_ST = SSM_BLOCK_GROUPS * SSM_STATE
SCAN_LANES = 512
STRIP = 64
ROUTER_EXPERT_LANE0 = 32


def _cparams(*sem):
    return pltpu.CompilerParams(dimension_semantics=sem, vmem_limit_bytes=VMEM_LIMIT)


def _sigmoid(x):
    return 1.0 / (1.0 + jnp.exp(-x))


def _tile(m, pref):
    t = min(m, pref)
    assert m % t == 0, (m, pref)
    return t


def _rms(x, g):
    return x * lax.rsqrt(jnp.mean(x * x, axis=-1, keepdims=True) + RMS_EPS) * g


def _rms_kernel(x_ref, g_ref, o_ref):
    o_ref[...] = _rms(x_ref[...], g_ref[...]).astype(o_ref.dtype)


def rmsnorm(x, g, out_dtype, tm=512):
    m, d = x.shape
    tm = _tile(m, tm)
    return pl.pallas_call(
        _rms_kernel,
        grid=(m // tm,),
        in_specs=[pl.BlockSpec((tm, d), lambda i: (i, 0)), pl.BlockSpec((1, d), lambda i: (0, 0))],
        out_specs=pl.BlockSpec((tm, d), lambda i: (i, 0)),
        out_shape=jax.ShapeDtypeStruct((m, d), out_dtype),
        compiler_params=_cparams("parallel"),
        name="rmsnorm",
    )(x, g.reshape(1, d))


def _mm_kernel(*refs, has_res):
    a_ref, b_ref = refs[0], refs[1]
    o_refs = refs[3:] if has_res else refs[2:]
    acc = jnp.dot(a_ref[...], b_ref[...], preferred_element_type=F32)
    if has_res:
        acc = refs[2][...] + acc
    for o_ref in o_refs:
        o_ref[...] = acc.astype(o_ref.dtype)


def matmul(a, b, out_dtypes, res=None, tm=512, tn=1024, name="matmul"):
    m, k = a.shape
    n = b.shape[1]
    tm, tn = _tile(m, tm), _tile(n, tn)
    in_specs = [pl.BlockSpec((tm, k), lambda i, j: (i, 0)), pl.BlockSpec((k, tn), lambda i, j: (0, j))]
    args = [a, b]
    if res is not None:
        in_specs.append(pl.BlockSpec((tm, tn), lambda i, j: (i, j)))
        args.append(res)
    outs = pl.pallas_call(
        functools.partial(_mm_kernel, has_res=res is not None),
        grid=(m // tm, n // tn),
        in_specs=in_specs,
        out_specs=[pl.BlockSpec((tm, tn), lambda i, j: (i, j)) for _ in out_dtypes],
        out_shape=[jax.ShapeDtypeStruct((m, n), dt) for dt in out_dtypes],
        compiler_params=_cparams("parallel", "arbitrary"),
        name=name,
    )(*args)
    return outs


def _proj_kernel(a_ref, b_ref, u_ref, q_ref, k_ref, kb_ref, v_ref, vb_ref, qm_ref, *, q_scale):
    j = pl.program_id(1)
    acc = jnp.dot(a_ref[...], b_ref[...], preferred_element_type=F32)

    @pl.when(j == 0)
    def _():
        u_ref[...] = acc

    @pl.when(j == 1)
    def _():
        q_ref[...] = (acc * q_scale).astype(BF16)

    @pl.when(j == 2)
    def _():
        k_ref[...] = acc
        kb_ref[...] = acc.astype(BF16)

    @pl.when(j == 3)
    def _():
        v_ref[...] = acc
        vb_ref[...] = acc.astype(BF16)

    @pl.when(j == 4)
    def _():
        qm_ref[...] = acc.astype(BF16)


def in_projection(hn, w5, width, tm=512):
    m, k = hn.shape
    tm = _tile(m, tm)
    dts = (F32, BF16, F32, BF16, F32, BF16, BF16)
    return pl.pallas_call(
        functools.partial(_proj_kernel, q_scale=-(SB_HEAD_DIM ** -0.5)),
        grid=(m // tm, 5),
        in_specs=[pl.BlockSpec((tm, k), lambda i, j: (i, 0)), pl.BlockSpec((k, width), lambda i, j: (0, j))],
        out_specs=[pl.BlockSpec((tm, width), lambda i, j: (i, 0)) for _ in dts],
        out_shape=[jax.ShapeDtypeStruct((m, width), dt) for dt in dts],
        compiler_params=_cparams("parallel", "arbitrary"),
        name="in_projection",
    )(hn, w5)


def _cmul_add(ar, ai, hr, hi, xr, xi):
    return ar * hr - ai * hi + xr, ar * hi + ai * hr + xi


def _s5_readout(q, x_ref, u_ref, cblk_ref, d_ref, z_ref):
    cols = slice(q * SSM_BLOCK_CH, (q + 1) * SSM_BLOCK_CH)
    y = jnp.dot(x_ref[...].astype(BF16), cblk_ref[q], preferred_element_type=F32)
    y = y + d_ref[:, cols] * u_ref[:, cols]
    z_ref[:, cols] = jax.nn.gelu(y)


def _s5_glu(z_ref, wglu_ref, bglu_ref):
    z = z_ref[...]
    gate = _sigmoid(jnp.dot(z.astype(BF16), wglu_ref[...], preferred_element_type=F32) + bglu_ref[...])
    return z * gate


def _s5_prompt_kernel(u_ref, h0r_ref, h0i_ref, bblk_ref, cblk_ref, d_ref, apr_ref, api_ref, wglu_ref, bglu_ref,
                      o_ref, hr_out_ref, hi_out_ref, x_ref, z_ref, hcr_ref, hci_ref, *, nj):
    n_blocks = bblk_ref.shape[0]

    @pl.when(pl.program_id(0) == 0)
    def _():
        hcr_ref[...] = h0r_ref[...]
        hci_ref[...] = h0i_ref[...]

    for q in range(n_blocks):
        ub = u_ref[:, q * SSM_BLOCK_CH:(q + 1) * SSM_BLOCK_CH]
        x_ref[...] = jnp.dot(ub.astype(BF16), bblk_ref[q], preferred_element_type=F32)
        for c in range(SSM_BLOCK_ST // SCAN_LANES):
            re = slice(c * SCAN_LANES, (c + 1) * SCAN_LANES)
            im = slice(SSM_BLOCK_ST + c * SCAN_LANES, SSM_BLOCK_ST + (c + 1) * SCAN_LANES)
            st = slice(q * SSM_BLOCK_ST + c * SCAN_LANES, q * SSM_BLOCK_ST + (c + 1) * SCAN_LANES)
            ar = jnp.broadcast_to(apr_ref[0:1, st], (SUBLANES, SCAN_LANES))
            ai = jnp.broadcast_to(api_ref[0:1, st], (SUBLANES, SCAN_LANES))

            def scan_body(j, carry, re=re, im=im, ar=ar, ai=ai):
                rows = pl.ds(pl.multiple_of(j * SUBLANES, SUBLANES), SUBLANES)
                hr, hi = _cmul_add(ar, ai, carry[0], carry[1], x_ref[rows, re], x_ref[rows, im])
                x_ref[rows, re] = hr
                x_ref[rows, im] = hi
                return hr, hi

            zero = jnp.zeros((SUBLANES, SCAN_LANES), F32)
            er, ei = lax.fori_loop(0, nj, scan_body, (zero, zero), unroll=2)

            alr, ali = apr_ref[nj - 1:nj, st], api_ref[nj - 1:nj, st]
            cr, ci = hcr_ref[:, st], hci_ref[:, st]
            crs, cis = [], []
            for s in range(SUBLANES):
                crs.append(cr)
                cis.append(ci)
                cr, ci = _cmul_add(alr, ali, cr, ci, er[s:s + 1, :], ei[s:s + 1, :])
            hcr_ref[:, st] = cr
            hci_ref[:, st] = ci
            cin_r = jnp.concatenate(crs, axis=0)
            cin_i = jnp.concatenate(cis, axis=0)

            def fix_body(j, _, re=re, im=im, st=st, cin_r=cin_r, cin_i=cin_i):
                rows = pl.ds(pl.multiple_of(j * SUBLANES, SUBLANES), SUBLANES)
                pr = jnp.broadcast_to(apr_ref[pl.ds(j, 1), st], (SUBLANES, SCAN_LANES))
                pi = jnp.broadcast_to(api_ref[pl.ds(j, 1), st], (SUBLANES, SCAN_LANES))
                hr, hi = _cmul_add(pr, pi, cin_r, cin_i, x_ref[rows, re], x_ref[rows, im])
                x_ref[rows, re] = hr
                x_ref[rows, im] = hi
                return 0

            lax.fori_loop(0, nj, fix_body, 0, unroll=2)
        _s5_readout(q, x_ref, u_ref, cblk_ref, d_ref, z_ref)

    o_ref[...] = _s5_glu(z_ref, wglu_ref, bglu_ref).astype(o_ref.dtype)
    hr_out_ref[...] = hcr_ref[...]
    hi_out_ref[...] = hci_ref[...]


def _s5_sample_kernel(u_ref, h0r_ref, h0i_ref, bblk_ref, cblk_ref, d_ref, apr_ref, api_ref, wglu_ref, bglu_ref,
                      o_ref, hr_out_ref, hi_out_ref, x_ref, z_ref, *, nb, nt):
    n_blocks = bblk_ref.shape[0]
    for q in range(n_blocks):
        ub = u_ref[:, q * SSM_BLOCK_CH:(q + 1) * SSM_BLOCK_CH]
        x_ref[...] = jnp.dot(ub.astype(BF16), bblk_ref[q], preferred_element_type=F32)
        for c in range(SSM_BLOCK_ST // SCAN_LANES):
            re = slice(c * SCAN_LANES, (c + 1) * SCAN_LANES)
            im = slice(SSM_BLOCK_ST + c * SCAN_LANES, SSM_BLOCK_ST + (c + 1) * SCAN_LANES)
            st = slice(q * SSM_BLOCK_ST + c * SCAN_LANES, q * SSM_BLOCK_ST + (c + 1) * SCAN_LANES)
            ar, ai = apr_ref[0:1, st], api_ref[0:1, st]
            hr, hi = h0r_ref[:, st], h0i_ref[:, st]
            for t in range(nt):
                rows = slice(t * nb, (t + 1) * nb)
                hr, hi = _cmul_add(ar, ai, hr, hi, x_ref[rows, re], x_ref[rows, im])
                x_ref[rows, re] = hr
                x_ref[rows, im] = hi
            hr_out_ref[:, st] = hr
            hi_out_ref[:, st] = hi
        _s5_readout(q, x_ref, u_ref, cblk_ref, d_ref, z_ref)
    o_ref[...] = _s5_glu(z_ref, wglu_ref, bglu_ref).astype(o_ref.dtype)


def _s5_params(lam_re, lam_im, log_step, b_re, b_im, c_re, c_im, d_skip, n_pow):
    g, p = lam_re.shape
    dt = jnp.exp(log_step)[:, None]
    steps = jnp.arange(1, n_pow + 1, dtype=F32)[:, None, None]
    mag = jnp.exp(lam_re * dt * steps)
    phase = lam_im * dt * steps
    ap_re = (mag * jnp.cos(phase)).reshape(n_pow, g * p)
    ap_im = (mag * jnp.sin(phase)).reshape(n_pow, g * p)
    a_re, a_im = mag[0] * jnp.cos(phase[0]), mag[0] * jnp.sin(phase[0])
    num_re = a_re - 1.0
    den = lam_re * lam_re + lam_im * lam_im
    k_re = (num_re * lam_re + a_im * lam_im) / den
    k_im = (a_im * lam_re - num_re * lam_im) / den
    bp_re = k_re[..., None] * b_re - k_im[..., None] * b_im
    bp_im = k_re[..., None] * b_im + k_im[..., None] * b_re
    nblk = g // SSM_BLOCK_GROUPS
    eye = jnp.eye(SSM_BLOCK_GROUPS, dtype=F32)

    def b_block(bp):
        bp = bp.reshape(nblk, SSM_BLOCK_GROUPS, p, SSM_GROUP)
        return jnp.einsum('qgpc,gh->qgchp', bp, eye).reshape(nblk, SSM_BLOCK_CH, SSM_BLOCK_ST)

    def c_block(c):
        c = c.reshape(nblk, SSM_BLOCK_GROUPS, SSM_GROUP, p)
        return jnp.einsum('qgcp,gh->qgphc', c, eye).reshape(nblk, SSM_BLOCK_ST, SSM_BLOCK_CH)

    bblk = jnp.concatenate([b_block(bp_re), b_block(bp_im)], axis=2).astype(BF16)
    cblk = jnp.concatenate([c_block(c_re), -c_block(c_im)], axis=1).astype(BF16)
    return bblk, cblk, d_skip.reshape(1, g * SSM_GROUP), ap_re, ap_im


def _const_spec(shape):
    return pl.BlockSpec(shape, lambda *_: (0,) * len(shape))


def s5_prompt(u, h0r, h0i, params, w_glu, b_glu, chunk):
    t, w = u.shape
    nj = chunk // SUBLANES
    bblk, cblk, d, apr, api = params
    ns = apr.shape[1]
    consts = [h0r, h0i, bblk, cblk, d, apr, api, w_glu, b_glu.reshape(1, w)]
    u_perm = u.reshape(t // chunk, SUBLANES, nj, w).transpose(0, 2, 1, 3).reshape(t, w)
    o_perm, h_re, h_im = pl.pallas_call(
        functools.partial(_s5_prompt_kernel, nj=nj),
        grid=(t // chunk,),
        in_specs=[pl.BlockSpec((chunk, w), lambda i: (i, 0))] + [_const_spec(c.shape) for c in consts],
        out_specs=[pl.BlockSpec((chunk, w), lambda i: (i, 0)), _const_spec((1, ns)), _const_spec((1, ns))],
        out_shape=[jax.ShapeDtypeStruct((t, w), BF16), jax.ShapeDtypeStruct((1, ns), F32),
                   jax.ShapeDtypeStruct((1, ns), F32)],
        scratch_shapes=[pltpu.VMEM((chunk, 2 * SSM_BLOCK_ST), F32), pltpu.VMEM((chunk, w), F32),
                        pltpu.VMEM((1, ns), F32), pltpu.VMEM((1, ns), F32)],
        compiler_params=_cparams("arbitrary"),
        name="s5_prompt",
    )(u_perm, *consts)
    o = o_perm.reshape(t // chunk, nj, SUBLANES, w).transpose(0, 2, 1, 3).reshape(t, w)
    return o, h_re, h_im


def s5_sample(u_tb, h0r, h0i, params, w_glu, b_glu, nb, nt):
    m, w = u_tb.shape
    bblk, cblk, d, apr, api = params
    ns = apr.shape[1]
    consts = [h0r, h0i, bblk, cblk, d, apr, api, w_glu, b_glu.reshape(1, w)]
    return pl.pallas_call(
        functools.partial(_s5_sample_kernel, nb=nb, nt=nt),
        grid=(1,),
        in_specs=[_const_spec(u_tb.shape)] + [_const_spec(c.shape) for c in consts],
        out_specs=[_const_spec((m, w)), _const_spec((nb, ns)), _const_spec((nb, ns))],
        out_shape=[jax.ShapeDtypeStruct((m, w), BF16), jax.ShapeDtypeStruct((nb, ns), F32),
                   jax.ShapeDtypeStruct((nb, ns), F32)],
        scratch_shapes=[pltpu.VMEM((m, 2 * SSM_BLOCK_ST), F32), pltpu.VMEM((m, w), F32)],
        compiler_params=_cparams("arbitrary"),
        name="s5_sample",
    )(u_tb, *consts)


def _log_sigmoid_pair(w):
    lk = jnp.minimum(w, 0.0) - jnp.log(1.0 + jnp.exp(-jnp.abs(w)))
    return lk - w, lk


def _split3_bf16(x):
    hi = x.astype(BF16).astype(F32)
    mid = (x - hi).astype(BF16).astype(F32)
    lo = (x - hi - mid).astype(BF16).astype(F32)
    return hi, mid, lo


def _sb_prompt_kernel(bias_ref, q_ref, k_ref, v_ref, o_ref, acc_ref, carry_ref, *, tq, tk, n_hp):
    hp, i = pl.program_id(0), pl.program_id(1)
    kb_per_q = tq // tk
    row = lax.broadcasted_iota(jnp.int32, (tq, tk), 0)
    col = lax.broadcasted_iota(jnp.int32, (tq, tk), 1)
    newer = (lax.broadcasted_iota(jnp.int32, (tk, tk), 0)
             > lax.broadcasted_iota(jnp.int32, (tk, tk), 1)).astype(BF16)
    acc_ref[...] = jnp.zeros_like(acc_ref)
    carry_ref[...] = jnp.zeros_like(carry_ref)

    lane_q = lax.broadcasted_iota(jnp.int32, (tq, SB_HEAD_DIM), 1)
    lane_k = lax.broadcasted_iota(jnp.int32, (tk, SB_HEAD_DIM), 1)
    q_units = jnp.where(lane_q < 3, 1.0, 0.0).astype(BF16)
    q_aug, k_bias = [], []
    for hh in range(n_hp):
        hi, mid, lo = _split3_bf16(jnp.full((tk, SB_HEAD_DIM), -bias_ref[hp * n_hp + hh], F32))
        k_bias.append(jnp.where(lane_k == 0, hi, jnp.where(lane_k == 1, mid, jnp.where(lane_k == 2, lo, 0.0)))
                      .astype(BF16))
        q_aug.append(jnp.concatenate([q_ref[:, hh * SB_HEAD_DIM:(hh + 1) * SB_HEAD_DIM], q_units], axis=1))

    def tile(kb, mask):
        rows = pl.ds(pl.multiple_of(kb * tk, tk), tk)
        for hh in range(n_hp):
            cols = slice(hh * SB_HEAD_DIM, (hh + 1) * SB_HEAD_DIM)
            k_aug = jnp.concatenate([k_ref[rows, cols], k_bias[hh]], axis=1)
            w = lax.dot_general(q_aug[hh], k_aug, (((1,), (1,)), ((), ())), preferred_element_type=F32)
            ls_s, lk_s, lkb_s = [], [], []
            for s in range(tq // STRIP):
                r = slice(s * STRIP, (s + 1) * STRIP)
                ls, lk = _log_sigmoid_pair(w[r])
                if mask is not None:
                    lk = jnp.where(mask[r], lk, 0.0)
                ls_s.append(ls)
                lk_s.append(jnp.sum(lk, axis=1, keepdims=True))
                lkb_s.append(lk.astype(BF16))
            cum = jnp.dot(jnp.concatenate(lkb_s, axis=0), newer, preferred_element_type=F32)
            a_s = []
            for s in range(tq // STRIP):
                r = slice(s * STRIP, (s + 1) * STRIP)
                a = jnp.exp(ls_s[s] + cum[r] + carry_ref[hh, r, :])
                if mask is not None:
                    a = jnp.where(mask[r], a, 0.0)
                a_s.append(a.astype(BF16))
            acc_ref[hh] += jnp.dot(jnp.concatenate(a_s, axis=0), v_ref[rows, cols], preferred_element_type=F32)
            carry_ref[hh] += jnp.concatenate(lk_s, axis=0)

    for d in reversed(range(kb_per_q)):
        tile(i * kb_per_q + d, col + d * tk < row)

    def body(jj, _):
        for d in range(kb_per_q):
            tile((i - jj) * kb_per_q - 1 - d, None)
        return 0

    lax.fori_loop(0, i, body, 0)
    for hh in range(n_hp):
        o_ref[:, hh * SB_HEAD_DIM:(hh + 1) * SB_HEAD_DIM] = acc_ref[hh].astype(o_ref.dtype)


def sb_attention_prompt(q, k, v, bias, tq=512, tk=256, heads_per_step=2):
    t, w = q.shape
    n_heads = w // SB_HEAD_DIM
    tq, tk, n_hp = _tile(t, tq), _tile(t, tk), heads_per_step
    assert tq % tk == 0 and n_heads % n_hp == 0
    wb = n_hp * SB_HEAD_DIM
    grid_spec = pltpu.PrefetchScalarGridSpec(
        num_scalar_prefetch=1,
        grid=(n_heads // n_hp, t // tq),
        in_specs=[pl.BlockSpec((tq, wb), lambda h, i, b: (i, h)),
                  pl.BlockSpec((t, wb), lambda h, i, b: (0, h)),
                  pl.BlockSpec((t, wb), lambda h, i, b: (0, h))],
        out_specs=pl.BlockSpec((tq, wb), lambda h, i, b: (i, h)),
        scratch_shapes=[pltpu.VMEM((n_hp, tq, SB_HEAD_DIM), F32), pltpu.VMEM((n_hp, tq, 1), F32)],
    )
    return pl.pallas_call(
        functools.partial(_sb_prompt_kernel, tq=tq, tk=tk, n_hp=n_hp),
        grid_spec=grid_spec,
        out_shape=jax.ShapeDtypeStruct((t, w), BF16),
        compiler_params=_cparams("parallel", "arbitrary"),
        name="sb_attention_prompt",
    )(bias, q, k, v)


def _sb_sample_kernel(pt_ref, *refs, n_pp, n_heads):
    wq_ref, bias_ref, kn_ref, vn_ref = refs[:4]
    kp_refs = refs[4:4 + n_pp]
    vp_refs = refs[4 + n_pp:4 + 2 * n_pp]
    o_ref, acc_ref, carry_ref = refs[4 + 2 * n_pp:]
    p = pl.program_id(1)
    tk = PAGE_SIZE
    nq = o_ref.shape[0]
    dh = SB_HEAD_DIM
    newer = (lax.broadcasted_iota(jnp.int32, (tk, tk), 1)
             > lax.broadcasted_iota(jnp.int32, (tk, tk), 0)).astype(BF16)

    def head_rows(ref, h):
        return ref[pl.ds(h, tk, stride=n_heads), :]

    def weights(z, mask, carry):
        ls, lk = _log_sigmoid_pair(z)
        if mask is not None:
            lk = jnp.where(mask, lk, 0.0)
        after = jnp.dot(newer, lk.astype(BF16), preferred_element_type=F32) + carry
        a = jnp.exp(ls + after)
        if mask is not None:
            a = jnp.where(mask, a, 0.0)
        return a.T, jnp.sum(lk, axis=0, keepdims=True)

    @pl.when(p == 0)
    def _():
        lane_q = lax.broadcasted_iota(jnp.int32, (tk, LANES), 1) % nq
        key = lax.broadcasted_iota(jnp.int32, (tk, LANES), 0)
        z = jnp.dot(kn_ref[...], wq_ref[...], preferred_element_type=F32) - bias_ref[...]
        at, tot = weights(z, key < lane_q, jnp.zeros((1, LANES), F32))
        carry_ref[...] = tot
        for h in range(n_heads):
            acc_ref[h * nq:(h + 1) * nq, :] = jnp.dot(at[h * nq:(h + 1) * nq, :].astype(BF16),
                                                      vn_ref[:, h * dh:(h + 1) * dh], preferred_element_type=F32)

    z = -bias_ref[...]
    for hp in range(n_heads // 2):
        lhs = jnp.concatenate(
            [jnp.concatenate([head_rows(kp_refs[r], 2 * hp), head_rows(kp_refs[r], 2 * hp + 1)], axis=1)
             for r in range(n_pp)], axis=0).astype(BF16)
        z = z + jnp.dot(lhs, wq_ref[2 * hp * dh:(2 * hp + 2) * dh, :], preferred_element_type=F32)
    carry = carry_ref[...]
    ats = []
    for r in range(n_pp):
        at, tot = weights(z[r * tk:(r + 1) * tk, :], None, carry)
        carry = carry + tot
        ats.append(at)
    carry_ref[...] = carry
    for h in range(n_heads):
        lhs = jnp.concatenate([at[h * nq:(h + 1) * nq, :] for at in ats], axis=1).astype(BF16)
        rhs = jnp.concatenate([head_rows(vp_refs[r], h) for r in range(n_pp)], axis=0).astype(BF16)
        acc_ref[h * nq:(h + 1) * nq, :] += jnp.dot(lhs, rhs, preferred_element_type=F32)

    @pl.when(p == pl.num_programs(1) - 1)
    def _():
        for h in range(n_heads):
            o_ref[:, h * dh:(h + 1) * dh] = acc_ref[h * nq:(h + 1) * nq, :]


def sb_attention_sample(q, k_new, v_new, bias, cache_k, cache_v, page_table, pages_per_step=8):
    nb, nq, w = q.shape
    n_heads = w // SB_HEAD_DIM
    n_pages = page_table.shape[1]
    n_pp = pages_per_step
    assert n_pages % n_pp == 0 and n_heads * nq <= LANES and nq <= PAGE_SIZE and n_heads % 2 == 0
    q4 = q.reshape(nb, nq, n_heads, SB_HEAD_DIM).transpose(0, 2, 3, 1)
    wq = jnp.einsum('bhdt,hg->bhdgt', q4, jnp.eye(n_heads, dtype=q.dtype)).reshape(nb, w, n_heads * nq)
    wq = jnp.pad(wq, ((0, 0), (0, 0), (0, LANES - n_heads * nq)))
    bias_row = jnp.pad(jnp.repeat(bias, nq), (0, LANES - n_heads * nq)).reshape(1, LANES)
    pad = ((0, 0), (0, PAGE_SIZE - nq), (0, 0))
    k_new, v_new = jnp.pad(k_new, pad), jnp.pad(v_new, pad)

    page_rows = PAGE_SIZE * n_heads

    def page_spec(r):
        return pl.BlockSpec((None, page_rows, SB_HEAD_DIM),
                            lambda b, p, pt: (pt[b, n_pages - 1 - (p * n_pp + r)], 0, 0))

    per_b = lambda b, p, pt: (b, 0, 0)
    grid_spec = pltpu.PrefetchScalarGridSpec(
        num_scalar_prefetch=1,
        grid=(nb, n_pages // n_pp),
        in_specs=[pl.BlockSpec((None, w, LANES), per_b), pl.BlockSpec((1, LANES), lambda b, p, pt: (0, 0)),
                  pl.BlockSpec((None, PAGE_SIZE, w), per_b), pl.BlockSpec((None, PAGE_SIZE, w), per_b)]
                 + [page_spec(r) for r in range(n_pp)] + [page_spec(r) for r in range(n_pp)],
        out_specs=pl.BlockSpec((None, nq, w), per_b),
        scratch_shapes=[pltpu.VMEM((n_heads * nq, SB_HEAD_DIM), F32), pltpu.VMEM((1, LANES), F32)],
    )
    return pl.pallas_call(
        functools.partial(_sb_sample_kernel, n_pp=n_pp, n_heads=n_heads),
        grid_spec=grid_spec,
        out_shape=jax.ShapeDtypeStruct((nb, nq, w), F32),
        compiler_params=_cparams("parallel", "arbitrary"),
        name="sb_attention_sample",
    )(page_table, wq, bias_row, k_new, v_new, *([cache_k] * n_pp), *([cache_v] * n_pp))


def _mem_attn_kernel(q_ref, k_ref, v_ref, o_ref, *, scale):
    dh = o_ref.shape[1] // MEM_HEADS
    for h in range(MEM_HEADS):
        cols = slice(h * dh, (h + 1) * dh)
        kh = k_ref[:, cols].astype(BF16)
        vh = v_ref[:, cols].astype(BF16)
        s = lax.dot_general(q_ref[:, cols], kh, (((1,), (1,)), ((), ())), preferred_element_type=F32) * scale
        e = jnp.exp(s - jnp.max(s, axis=-1, keepdims=True))
        p = e / jnp.sum(e, axis=-1, keepdims=True)
        o_ref[:, cols] = jnp.dot(p.astype(BF16), vh, preferred_element_type=F32).astype(o_ref.dtype)


def memory_attention(q, mem_k, mem_v, tq=512):
    nb, t, w = q.shape
    n_mem = mem_k.shape[1]
    tq = _tile(t, tq)
    return pl.pallas_call(
        functools.partial(_mem_attn_kernel, scale=(w // MEM_HEADS) ** -0.5),
        grid=(nb, t // tq),
        in_specs=[pl.BlockSpec((None, tq, w), lambda b, i: (b, i, 0)),
                  pl.BlockSpec((None, n_mem, w), lambda b, i: (b, 0, 0)),
                  pl.BlockSpec((None, n_mem, w), lambda b, i: (b, 0, 0))],
        out_specs=pl.BlockSpec((None, tq, w), lambda b, i: (b, i, 0)),
        out_shape=jax.ShapeDtypeStruct((nb, t, w), BF16),
        compiler_params=_cparams("parallel", "arbitrary"),
        name="memory_attention",
    )(q, mem_k, mem_v)


def _merge_kernel(hn_ref, o0_ref, o1_ref, o2_ref, g0_ref, g1_ref, g2_ref, b0_ref, b1_ref, b2_ref, out_ref):
    hn = hn_ref[...]
    total = None
    for o_ref, g_ref, b_ref in ((o0_ref, g0_ref, b0_ref), (o1_ref, g1_ref, b1_ref), (o2_ref, g2_ref, b2_ref)):
        gate = _sigmoid(jnp.dot(hn, g_ref[...], preferred_element_type=F32))
        term = gate * jnp.dot(o_ref[...], b_ref[...], preferred_element_type=F32)
        total = term if total is None else total + term
    out_ref[...] = total.astype(out_ref.dtype)


def gated_merge(hn, branches, w_gates, w_branches, tm=512, tn=512):
    m, d = hn.shape
    tm, tn = _tile(m, tm), _tile(d, tn)
    nj = d // tn
    row = lambda i, j: (i, 0)
    in_specs = [pl.BlockSpec((tm, d), row)]
    in_specs += [pl.BlockSpec((tm, o.shape[1]), row) for o in branches]
    in_specs += [pl.BlockSpec((d, tn), functools.partial(lambda i, j, b: (0, b * nj + j), b=b)) for b in range(3)]
    in_specs += [pl.BlockSpec((w.shape[0], tn), lambda i, j: (0, j)) for w in w_branches]
    return pl.pallas_call(
        _merge_kernel,
        grid=(m // tm, nj),
        in_specs=in_specs,
        out_specs=pl.BlockSpec((tm, tn), lambda i, j: (i, j)),
        out_shape=jax.ShapeDtypeStruct((m, d), BF16),
        compiler_params=_cparams("parallel", "arbitrary"),
        name="gated_merge",
    )(hn, *branches, w_gates, w_gates, w_gates, *w_branches)


MOE_TILE = 1024
MOE_ROWS = 256
MOE_EXPERTS_PER_STEP = 2
ROUTER_GROUP_LANE = 64
ROUTER_RANK_LANE = 65


def _router_kernel(x_ref, g_ref, wr_ref, br_ref, hn_ref, gate_ref, cnt_ref, run_ref):
    hn = _rms(x_ref[...], g_ref[...])
    hn_ref[...] = hn.astype(hn_ref.dtype)
    logits = jnp.dot(hn, wr_ref[...], preferred_element_type=F32, precision=lax.Precision.HIGHEST) + br_ref[...]
    lane = lax.broadcasted_iota(jnp.int32, logits.shape, 1).astype(F32)
    neg, far = -1e30, 1e9
    rmax = lambda v: jnp.max(v, axis=-1, keepdims=True)
    rmin = lambda v: jnp.min(v, axis=-1, keepdims=True)
    rsum = lambda v: jnp.sum(v, axis=-1, keepdims=True)
    is_group = lane < N_GROUPS
    gl = jnp.where(is_group, logits, neg)
    gmax = rmax(gl)
    g_val = 1.0 / rsum(jnp.where(is_group, jnp.exp(gl - gmax), 0.0))
    g_idx = rmin(jnp.where(gl == gmax, lane, far))
    lo = ROUTER_EXPERT_LANE0 + EXPERTS_PER_GROUP * g_idx
    in_group = jnp.logical_and(lane >= lo, lane < lo + EXPERTS_PER_GROUP)
    el = jnp.where(in_group, logits, neg)
    ex = jnp.where(in_group, jnp.exp(el - rmax(el)), 0.0)
    prob = jnp.where(in_group, ex / rsum(ex), -1.0)
    v1 = rmax(prob)
    i1 = rmin(jnp.where(prob == v1, lane, far))
    rest = jnp.where(lane == i1, -1.0, prob)
    v2 = rmax(rest)
    i2 = rmin(jnp.where(rest == v2, lane, far))
    within = jnp.where(lane == i1, v1 / (v1 + v2), jnp.where(lane == i2, v2 / (v1 + v2), 0.0))

    @pl.when(pl.program_id(1) == 0)
    def _():
        run_ref[...] = jnp.zeros_like(run_ref)

    tm = logits.shape[0]
    chose = jnp.where(lane == g_idx, 1.0, 0.0)
    earlier = (lax.broadcasted_iota(jnp.int32, (tm, tm), 1)
               < lax.broadcasted_iota(jnp.int32, (tm, tm), 0)).astype(BF16)
    before = jnp.dot(earlier, chose.astype(BF16), preferred_element_type=F32) + run_ref[...]
    rank = rsum(chose * before)
    run_ref[...] += jnp.sum(chose, axis=0, keepdims=True)
    cnt_ref[...] = run_ref[...]
    gate_ref[...] = (g_val * within + jnp.where(lane == ROUTER_GROUP_LANE, g_idx, 0.0)
                     + jnp.where(lane == ROUTER_RANK_LANE, rank, 0.0))


def moe_router(x, g, w_router, b_router, tile, tm=256):
    m, d = x.shape
    tm = _tile(tile, tm)
    sub = tile // tm
    row = lambda i, s: (i * sub + s, 0)
    return pl.pallas_call(
        _router_kernel,
        grid=(m // tile, sub),
        in_specs=[pl.BlockSpec((tm, d), row), _const_spec((1, d)), _const_spec((d, LANES)), _const_spec((1, LANES))],
        out_specs=[pl.BlockSpec((tm, d), row), pl.BlockSpec((tm, LANES), row),
                   pl.BlockSpec((None, 1, LANES), lambda i, s: (i, 0, 0))],
        out_shape=[jax.ShapeDtypeStruct((m, d), BF16), jax.ShapeDtypeStruct((m, LANES), F32),
                   jax.ShapeDtypeStruct((m // tile, 1, LANES), F32)],
        scratch_shapes=[pltpu.VMEM((1, LANES), F32)],
        compiler_params=_cparams("parallel", "arbitrary"),
        name="moe_router",
    )(x, g.reshape(1, d), w_router, b_router)


def _experts_kernel(bg_ref, nb_ref, hn_ref, gate_ref, x_ref, drow_ref, dcol_ref, wgu_ref, wd_ref, gf_ref, o_ref,
                    xs_ref, gs_ref, yb_ref):
    i, b, j = pl.program_id(0), pl.program_id(1), pl.program_id(2)
    tile = hn_ref.shape[0]
    rows = xs_ref.shape[0]
    ff = wd_ref.shape[1]
    active = b < nb_ref[i]
    last_j = j == pl.num_programs(2) - 1

    @pl.when(jnp.logical_and(b == 0, j == 0))
    def _():
        o_ref[...] = x_ref[...]

    @pl.when(jnp.logical_and(active, j == 0))
    def _():
        slot = lax.broadcasted_iota(jnp.int32, (rows, tile), 0) + b * rows
        pick = jnp.where(drow_ref[...] == slot, 1.0, 0.0).astype(BF16)
        xs_ref[...] = jnp.dot(pick, hn_ref[...], preferred_element_type=F32).astype(BF16)
        gs = None
        for part in _split3_bf16(gate_ref[...]):
            term = jnp.dot(pick, part.astype(BF16), preferred_element_type=F32)
            gs = term if gs is None else gs + term
        gs_ref[...] = gs
        yb_ref[...] = jnp.zeros_like(yb_ref)

    @pl.when(active)
    def _():
        lane = lax.broadcasted_iota(jnp.int32, gs_ref.shape, 1)
        y = yb_ref[...]
        for k in range(MOE_EXPERTS_PER_STEP):
            e_lane = ROUTER_EXPERT_LANE0 + bg_ref[i, b] * EXPERTS_PER_GROUP + j * MOE_EXPERTS_PER_STEP + k
            gate = jnp.sum(jnp.where(lane == e_lane, gs_ref[...], 0.0), axis=-1, keepdims=True)
            au = jnp.dot(xs_ref[...], wgu_ref[k], preferred_element_type=F32)
            a, up = au[:, :ff], au[:, ff:]
            act = a * _sigmoid(a) * up * gate
            y = y + jnp.dot(act.astype(BF16), wd_ref[k], preferred_element_type=F32)
        yb_ref[...] = y

    @pl.when(jnp.logical_and(active, last_j))
    def _():
        slot = lax.broadcasted_iota(jnp.int32, (tile, rows), 1) + b * rows
        put = jnp.where(dcol_ref[...] == slot, 1.0, 0.0).astype(BF16)
        y = yb_ref[...]
        y_hi = y.astype(BF16)
        y_lo = (y - y_hi.astype(F32)).astype(BF16)
        o_ref[...] += (jnp.dot(put, y_hi, preferred_element_type=F32)
                       + jnp.dot(put, y_lo, preferred_element_type=F32))

    @pl.when(jnp.logical_and(b == pl.num_programs(1) - 1, last_j))
    def _():
        o_ref[...] = _rms(o_ref[...], gf_ref[...])


def moe_experts_final_norm(hn, route, counts, x, w_gate_up, w_down, g_final, tile):
    m, d = x.shape
    n_exp, _, ff2 = w_gate_up.shape
    n_tiles = m // tile
    rows = min(MOE_ROWS, tile)
    n_blocks = tile // rows + N_GROUPS - 1
    per = MOE_EXPERTS_PER_STEP
    steps = EXPERTS_PER_GROUP // per
    w_gate_up = w_gate_up.reshape(n_exp // per, per, d, ff2)
    w_down = w_down.reshape(n_exp // per, per, ff2 // 2, d)
    cnt = counts[:, 0, :N_GROUPS].astype(jnp.int32)
    blocks_g = (cnt + rows - 1) // rows
    end_g = jnp.cumsum(blocks_g, axis=1)
    start_g = end_g - blocks_g
    n_blk = end_g[:, -1]
    bidx = jnp.arange(n_blocks, dtype=jnp.int32)[None, :]
    blk_group = jnp.sum(bidx[:, :, None] >= end_g[:, None, :], axis=2).astype(jnp.int32)
    blk_group = jnp.minimum(blk_group, jnp.take_along_axis(blk_group, jnp.maximum(n_blk - 1, 0)[:, None], axis=1))
    grp = route[:, ROUTER_GROUP_LANE].astype(jnp.int32).reshape(n_tiles, tile)
    rank = route[:, ROUTER_RANK_LANE].astype(jnp.int32).reshape(n_tiles, tile)
    dest = rows * jnp.take_along_axis(start_g, grp, axis=1) + rank

    def w_index(i, b, j, bg, nb):
        live = b < nb[i]
        last = jnp.maximum(nb[i] - 1, 0)
        return (jnp.where(live, bg[i, b] * steps + j, bg[i, last] * steps + steps - 1), 0, 0, 0)

    row = lambda i, b, j, bg, nb: (i, 0)
    grid_spec = pltpu.PrefetchScalarGridSpec(
        num_scalar_prefetch=2,
        grid=(n_tiles, n_blocks, steps),
        in_specs=[pl.BlockSpec((tile, d), row, pipeline_mode=pl.Buffered(1)), pl.BlockSpec((tile, LANES), row),
                  pl.BlockSpec((tile, d), row, pipeline_mode=pl.Buffered(1)),
                  pl.BlockSpec((None, 1, tile), lambda i, b, j, bg, nb: (i, 0, 0)),
                  pl.BlockSpec((tile, 1), row),
                  pl.BlockSpec((None, per, d, ff2), w_index), pl.BlockSpec((None, per, ff2 // 2, d), w_index),
                  pl.BlockSpec((1, d), lambda i, b, j, bg, nb: (0, 0))],
        out_specs=pl.BlockSpec((tile, d), row),
        scratch_shapes=[pltpu.VMEM((rows, d), BF16), pltpu.VMEM((rows, LANES), F32), pltpu.VMEM((rows, d), F32)],
    )
    return pl.pallas_call(
        _experts_kernel,
        grid_spec=grid_spec,
        out_shape=jax.ShapeDtypeStruct((m, d), F32),
        compiler_params=_cparams("parallel", "arbitrary", "arbitrary"),
        name="moe_experts",
    )(blk_group, n_blk, hn, route, x, dest.reshape(n_tiles, 1, tile), dest.reshape(m, 1), w_gate_up, w_down,
      g_final.reshape(1, d))


S5_CHUNK = 512


def kernel(x_prompt, x_sample, mem_prompt, cache_k, cache_v, page_table, cache_mem_k, cache_mem_v, state_ssm_re, state_ssm_im, g_mix_norm, w_in, sb_logit_bias, ssm_lambda_re, ssm_lambda_im, ssm_log_step, ssm_b_re, ssm_b_im, ssm_c_re, ssm_c_im, ssm_d, w_glu, b_glu, g_mem_norm, w_mem_k, w_mem_v, w_branch_ssm, w_branch_sb, w_branch_mem, w_out, g_ffn_norm, w_group_router, b_group_router, w_expert_router, b_expert_router, w_exp_gate, w_exp_up, w_exp_down, g_final):
    depth = w_in.shape[0]
    assert depth == 1, "one mixer + MoE layer per step"
    bp, tp, d = x_prompt.shape
    bs, ts, _ = x_sample.shape
    assert bp == 1
    n_groups, n_state = ssm_lambda_re.shape[1:]
    ssm_w = n_groups * SSM_GROUP
    n_st = n_groups * n_state
    sb_w = cache_k.shape[3] * cache_k.shape[4]
    mem_w = cache_mem_k.shape[3] * cache_mem_k.shape[4]
    n_mem = mem_prompt.shape[1]
    assert ssm_w == sb_w == mem_w
    n_heads = sb_w // SB_HEAD_DIM
    mem_dh = mem_w // MEM_HEADS
    l = 0

    w5 = w_in[l, :, :5 * ssm_w].astype(BF16)
    w_gates = w_in[l, :, 5 * ssm_w:].astype(BF16)
    w_branches = [w_branch_ssm[l].astype(BF16), w_branch_sb[l].astype(BF16), w_branch_mem[l].astype(BF16)]
    w_out_b = w_out[l].astype(BF16)
    w_glu_b = w_glu[l].astype(BF16)
    w_memkv = jnp.concatenate([w_mem_k[l], w_mem_v[l]], axis=1).astype(BF16)
    w_gate_up = jnp.concatenate([w_exp_gate[l], w_exp_up[l]], axis=2).astype(BF16)
    w_down = w_exp_down[l].astype(BF16)
    w_router = jnp.zeros((d, LANES), F32)
    w_router = w_router.at[:, :N_GROUPS].set(w_group_router[l])
    w_router = w_router.at[:, ROUTER_EXPERT_LANE0:ROUTER_EXPERT_LANE0 + N_EXPERTS].set(w_expert_router[l])
    b_router = jnp.zeros((1, LANES), F32)
    b_router = b_router.at[0, :N_GROUPS].set(b_group_router[l])
    b_router = b_router.at[0, ROUTER_EXPERT_LANE0:ROUTER_EXPERT_LANE0 + N_EXPERTS].set(b_expert_router[l])
    s5p = _s5_params(ssm_lambda_re[l], ssm_lambda_im[l], ssm_log_step[l], ssm_b_re[l], ssm_b_im[l],
                     ssm_c_re[l], ssm_c_im[l], ssm_d[l], S5_CHUNK // SUBLANES)
    bias = sb_logit_bias[l]

    mn = rmsnorm(mem_prompt.reshape(n_mem, d), g_mem_norm[l], BF16)
    (mem_kv,) = matmul(mn, w_memkv, (F32,), tm=n_mem, tn=1024, name="memory_kv")
    mk_p, mv_p = mem_kv[:, :mem_w], mem_kv[:, mem_w:]

    def mixer_and_moe(x, attend, mem_k, mem_v, s5):
        hn = rmsnorm(x, g_mix_norm[l], BF16)
        u, q, k, kb, v, vb, qm = in_projection(hn, w5, ssm_w)
        o_ssm, h_re, h_im = s5(u)
        o_sb = attend(q, kb, vb)
        o_mem = memory_attention(qm.reshape(mem_k.shape[0], -1, mem_w), mem_k, mem_v).reshape(-1, mem_w)
        merged = gated_merge(hn, (o_ssm, o_sb, o_mem), w_gates, w_branches)
        (x2,) = matmul(merged, w_out_b, (F32,), res=x, tm=512, tn=1024, name="out_projection")
        moe_tile = min(MOE_TILE, x2.shape[0])
        hn2, route, counts = moe_router(x2, g_ffn_norm[l], w_router, b_router, moe_tile)
        y = moe_experts_final_norm(hn2, route, counts, x2, w_gate_up, w_down, g_final, moe_tile)
        return y, k, v, h_re, h_im

    zeros = jnp.zeros((1, n_st), F32)
    y_p, k_p, v_p, hr_p, hi_p = mixer_and_moe(
        x_prompt.reshape(tp, d),
        lambda q, kb, vb: sb_attention_prompt(q, kb, vb, bias),
        mk_p.reshape(1, n_mem, mem_w), mv_p.reshape(1, n_mem, mem_w),
        lambda u: s5_prompt(u, zeros, zeros, s5p, w_glu_b, b_glu[l], S5_CHUNK))

    def s5_s(u):
        u_tb = u.reshape(bs, ts, ssm_w).transpose(1, 0, 2).reshape(ts * bs, ssm_w)
        o_tb, h_re, h_im = s5_sample(u_tb, state_ssm_re[l].reshape(bs, n_st), state_ssm_im[l].reshape(bs, n_st),
                                     s5p, w_glu_b, b_glu[l], bs, ts)
        return o_tb.reshape(ts, bs, ssm_w).transpose(1, 0, 2).reshape(bs * ts, ssm_w), h_re, h_im

    def attend_s(q, kb, vb):
        o = sb_attention_sample(q.reshape(bs, ts, sb_w), kb.reshape(bs, ts, sb_w), vb.reshape(bs, ts, sb_w), bias,
                                cache_k.reshape(-1, PAGE_SIZE * n_heads, SB_HEAD_DIM),
                                cache_v.reshape(-1, PAGE_SIZE * n_heads, SB_HEAD_DIM),
                                page_table + l * cache_k.shape[1])
        return o.reshape(bs * ts, sb_w).astype(BF16)

    y_s, k_s, v_s, hr_s, hi_s = mixer_and_moe(
        x_sample.reshape(bs * ts, d), attend_s,
        cache_mem_k[l].reshape(bs, n_mem, mem_w), cache_mem_v[l].reshape(bs, n_mem, mem_w), s5_s)

    return (y_p.reshape(bp, tp, d), y_s.reshape(bs, ts, d),
            k_p.reshape(1, bp, tp, n_heads, SB_HEAD_DIM), v_p.reshape(1, bp, tp, n_heads, SB_HEAD_DIM),
            k_s.reshape(1, bs, ts, n_heads, SB_HEAD_DIM), v_s.reshape(1, bs, ts, n_heads, SB_HEAD_DIM),
            hr_p.reshape(1, bp, n_groups, n_state), hi_p.reshape(1, bp, n_groups, n_state),
            hr_s.reshape(1, bs, n_groups, n_state), hi_s.reshape(1, bs, n_groups, n_state),
            mk_p.reshape(1, bp, n_mem, MEM_HEADS, mem_dh), mv_p.reshape(1, bp, n_mem, MEM_HEADS, mem_dh))
```

```python
import functools
import math

import jax
import jax.numpy as jnp
from jax import lax
from jax.experimental import pallas as pl
from jax.experimental.pallas import tpu as pltpu

F32 = jnp.float32
BF16 = jnp.bfloat16

V7X_VMEM_BYTES = 64 * 1024 * 1024
VMEM_LIMIT = V7X_VMEM_BYTES - 8 * 1024 * 1024
LANES = 128
SUBLANES = 8

SSM_GROUP = 16
SSM_STATE = 64
SB_HEAD_DIM = 128
MEM_HEADS = 4
PAGE_SIZE = 128
N_GROUPS = 4
EXPERTS_PER_GROUP = 8
N_EXPERTS = N_GROUPS * EXPERTS_PER_GROUP
RMS_EPS = 1e-6

SSM_BLOCK_GROUPS = 16
SSM_BLOCK_CH = SSM_BLOCK_GROUPS * SSM_GROUP
SSM_BLOCK_ST = SSM_BLOCK_GROUPS * SSM_STATE
SCAN_LANES = 512
STRIP = 64
ROUTER_EXPERT_LANE0 = 32


def _cparams(*sem):
    return pltpu.CompilerParams(dimension_semantics=sem, vmem_limit_bytes=VMEM_LIMIT)


def _sigmoid(x):
    return 1.0 / (1.0 + jnp.exp(-x))


def _tile(m, pref):
    t = min(m, pref)
    assert m % t == 0, (m, pref)
    return t


def _rms(x, g):
    return x * lax.rsqrt(jnp.mean(x * x, axis=-1, keepdims=True) + RMS_EPS) * g


def _rms_kernel(x_ref, g_ref, o_ref):
    o_ref[...] = _rms(x_ref[...], g_ref[...]).astype(o_ref.dtype)


def rmsnorm(x, g, out_dtype, tm=512):
    m, d = x.shape
    tm = _tile(m, tm)
    return pl.pallas_call(
        _rms_kernel,
        grid=(m // tm,),
        in_specs=[pl.BlockSpec((tm, d), lambda i: (i, 0)), pl.BlockSpec((1, d), lambda i: (0, 0))],
        out_specs=pl.BlockSpec((tm, d), lambda i: (i, 0)),
        out_shape=jax.ShapeDtypeStruct((m, d), out_dtype),
        compiler_params=_cparams("parallel"),
        name="rmsnorm",
    )(x, g.reshape(1, d))


def _mm_kernel(*refs, has_res):
    a_ref, b_ref = refs[0], refs[1]
    o_refs = refs[3:] if has_res else refs[2:]
    acc = jnp.dot(a_ref[...], b_ref[...], preferred_element_type=F32)
    if has_res:
        acc = refs[2][...] + acc
    for o_ref in o_refs:
        o_ref[...] = acc.astype(o_ref.dtype)


def matmul(a, b, out_dtypes, res=None, tm=512, tn=1024, name="matmul"):
    m, k = a.shape
    n = b.shape[1]
    tm, tn = _tile(m, tm), _tile(n, tn)
    in_specs = [pl.BlockSpec((tm, k), lambda i, j: (i, 0)), pl.BlockSpec((k, tn), lambda i, j: (0, j))]
    args = [a, b]
    if res is not None:
        in_specs.append(pl.BlockSpec((tm, tn), lambda i, j: (i, j)))
        args.append(res)
    outs = pl.pallas_call(
        functools.partial(_mm_kernel, has_res=res is not None),
        grid=(m // tm, n // tn),
        in_specs=in_specs,
        out_specs=[pl.BlockSpec((tm, tn), lambda i, j: (i, j)) for _ in out_dtypes],
        out_shape=[jax.ShapeDtypeStruct((m, n), dt) for dt in out_dtypes],
        compiler_params=_cparams("parallel", "arbitrary"),
        name=name,
    )(*args)
    return outs


def _proj_kernel(a_ref, b_ref, u_ref, q_ref, k_ref, kb_ref, v_ref, vb_ref, qm_ref, *, q_scale):
    j = pl.program_id(1)
    acc = jnp.dot(a_ref[...], b_ref[...], preferred_element_type=F32)

    @pl.when(j == 0)
    def _():
        u_ref[...] = acc

    @pl.when(j == 1)
    def _():
        q_ref[...] = (acc * q_scale).astype(BF16)

    @pl.when(j == 2)
    def _():
        k_ref[...] = acc
        kb_ref[...] = acc.astype(BF16)

    @pl.when(j == 3)
    def _():
        v_ref[...] = acc
        vb_ref[...] = acc.astype(BF16)

    @pl.when(j == 4)
    def _():
        qm_ref[...] = acc.astype(BF16)


def in_projection(hn, w5, width, tm=512):
    m, k = hn.shape
    tm = _tile(m, tm)
    dts = (F32, BF16, F32, BF16, F32, BF16, BF16)
    return pl.pallas_call(
        functools.partial(_proj_kernel, q_scale=-(SB_HEAD_DIM ** -0.5)),
        grid=(m // tm, 5),
        in_specs=[pl.BlockSpec((tm, k), lambda i, j: (i, 0)), pl.BlockSpec((k, width), lambda i, j: (0, j))],
        out_specs=[pl.BlockSpec((tm, width), lambda i, j: (i, 0)) for _ in dts],
        out_shape=[jax.ShapeDtypeStruct((m, width), dt) for dt in dts],
        compiler_params=_cparams("parallel", "arbitrary"),
        name="in_projection",
    )(hn, w5)


def _cmul_add(ar, ai, hr, hi, xr, xi):
    return ar * hr - ai * hi + xr, ar * hi + ai * hr + xi


def _s5_readout(q, x_ref, u_ref, cblk_ref, d_ref, z_ref):
    cols = slice(q * SSM_BLOCK_CH, (q + 1) * SSM_BLOCK_CH)
    y = jnp.dot(x_ref[...].astype(BF16), cblk_ref[q], preferred_element_type=F32)
    y = y + d_ref[:, cols] * u_ref[:, cols]
    z_ref[:, cols] = jax.nn.gelu(y)


def _s5_glu(z_ref, wglu_ref, bglu_ref):
    z = z_ref[...]
    gate = _sigmoid(jnp.dot(z.astype(BF16), wglu_ref[...], preferred_element_type=F32) + bglu_ref[...])
    return z * gate


def _s5_prompt_kernel(u_ref, h0r_ref, h0i_ref, bblk_ref, cblk_ref, d_ref, apr_ref, api_ref, wglu_ref, bglu_ref,
                      o_ref, hr_out_ref, hi_out_ref, x_ref, z_ref, hcr_ref, hci_ref, *, nj):
    n_blocks = bblk_ref.shape[0]

    @pl.when(pl.program_id(0) == 0)
    def _():
        hcr_ref[...] = h0r_ref[...]
        hci_ref[...] = h0i_ref[...]

    for q in range(n_blocks):
        ub = u_ref[:, q * SSM_BLOCK_CH:(q + 1) * SSM_BLOCK_CH]
        x_ref[...] = jnp.dot(ub.astype(BF16), bblk_ref[q], preferred_element_type=F32)
        for c in range(SSM_BLOCK_ST // SCAN_LANES):
            re = slice(c * SCAN_LANES, (c + 1) * SCAN_LANES)
            im = slice(SSM_BLOCK_ST + c * SCAN_LANES, SSM_BLOCK_ST + (c + 1) * SCAN_LANES)
            st = slice(q * SSM_BLOCK_ST + c * SCAN_LANES, q * SSM_BLOCK_ST + (c + 1) * SCAN_LANES)
            ar = jnp.broadcast_to(apr_ref[0:1, st], (SUBLANES, SCAN_LANES))
            ai = jnp.broadcast_to(api_ref[0:1, st], (SUBLANES, SCAN_LANES))

            def scan_body(j, carry, re=re, im=im, ar=ar, ai=ai):
                rows = pl.ds(pl.multiple_of(j * SUBLANES, SUBLANES), SUBLANES)
                hr, hi = _cmul_add(ar, ai, carry[0], carry[1], x_ref[rows, re], x_ref[rows, im])
                x_ref[rows, re] = hr
                x_ref[rows, im] = hi
                return hr, hi

            zero = jnp.zeros((SUBLANES, SCAN_LANES), F32)
            er, ei = lax.fori_loop(0, nj, scan_body, (zero, zero), unroll=2)

            alr, ali = apr_ref[nj - 1:nj, st], api_ref[nj - 1:nj, st]
            cr, ci = hcr_ref[:, st], hci_ref[:, st]
            crs, cis = [], []
            for s in range(SUBLANES):
                crs.append(cr)
                cis.append(ci)
                cr, ci = _cmul_add(alr, ali, cr, ci, er[s:s + 1, :], ei[s:s + 1, :])
            hcr_ref[:, st] = cr
            hci_ref[:, st] = ci
            cin_r = jnp.concatenate(crs, axis=0)
            cin_i = jnp.concatenate(cis, axis=0)

            def fix_body(j, _, re=re, im=im, st=st, cin_r=cin_r, cin_i=cin_i):
                rows = pl.ds(pl.multiple_of(j * SUBLANES, SUBLANES), SUBLANES)
                pr = jnp.broadcast_to(apr_ref[pl.ds(j, 1), st], (SUBLANES, SCAN_LANES))
                pi = jnp.broadcast_to(api_ref[pl.ds(j, 1), st], (SUBLANES, SCAN_LANES))
                hr, hi = _cmul_add(pr, pi, cin_r, cin_i, x_ref[rows, re], x_ref[rows, im])
                x_ref[rows, re] = hr
                x_ref[rows, im] = hi
                return 0

            lax.fori_loop(0, nj, fix_body, 0, unroll=2)
        _s5_readout(q, x_ref, u_ref, cblk_ref, d_ref, z_ref)

    o_ref[...] = _s5_glu(z_ref, wglu_ref, bglu_ref).astype(o_ref.dtype)
    hr_out_ref[...] = hcr_ref[...]
    hi_out_ref[...] = hci_ref[...]


def _s5_sample_kernel(u_ref, h0r_ref, h0i_ref, bblk_ref, cblk_ref, d_ref, apr_ref, api_ref, wglu_ref, bglu_ref,
                      o_ref, hr_out_ref, hi_out_ref, x_ref, z_ref, *, nb, nt):
    n_blocks = bblk_ref.shape[0]
    for q in range(n_blocks):
        ub = u_ref[:, q * SSM_BLOCK_CH:(q + 1) * SSM_BLOCK_CH]
        x_ref[...] = jnp.dot(ub.astype(BF16), bblk_ref[q], preferred_element_type=F32)
        for c in range(SSM_BLOCK_ST // SCAN_LANES):
            re = slice(c * SCAN_LANES, (c + 1) * SCAN_LANES)
            im = slice(SSM_BLOCK_ST + c * SCAN_LANES, SSM_BLOCK_ST + (c + 1) * SCAN_LANES)
            st = slice(q * SSM_BLOCK_ST + c * SCAN_LANES, q * SSM_BLOCK_ST + (c + 1) * SCAN_LANES)
            ar, ai = apr_ref[0:1, st], api_ref[0:1, st]
            hr, hi = h0r_ref[:, st], h0i_ref[:, st]
            for t in range(nt):
                rows = slice(t * nb, (t + 1) * nb)
                hr, hi = _cmul_add(ar, ai, hr, hi, x_ref[rows, re], x_ref[rows, im])
                x_ref[rows, re] = hr
                x_ref[rows, im] = hi
            hr_out_ref[:, st] = hr
            hi_out_ref[:, st] = hi
        _s5_readout(q, x_ref, u_ref, cblk_ref, d_ref, z_ref)
    o_ref[...] = _s5_glu(z_ref, wglu_ref, bglu_ref).astype(o_ref.dtype)


def _s5_params(lam_re, lam_im, log_step, b_re, b_im, c_re, c_im, d_skip, n_pow):
    g, p = lam_re.shape
    dt = jnp.exp(log_step)[:, None]
    steps = jnp.arange(1, n_pow + 1, dtype=F32)[:, None, None]
    mag = jnp.exp(lam_re * dt * steps)
    phase = lam_im * dt * steps
    ap_re = (mag * jnp.cos(phase)).reshape(n_pow, g * p)
    ap_im = (mag * jnp.sin(phase)).reshape(n_pow, g * p)
    a_re, a_im = mag[0] * jnp.cos(phase[0]), mag[0] * jnp.sin(phase[0])
    num_re = a_re - 1.0
    den = lam_re * lam_re + lam_im * lam_im
    k_re = (num_re * lam_re + a_im * lam_im) / den
    k_im = (a_im * lam_re - num_re * lam_im) / den
    bp_re = k_re[..., None] * b_re - k_im[..., None] * b_im
    bp_im = k_re[..., None] * b_im + k_im[..., None] * b_re
    nblk = g // SSM_BLOCK_GROUPS
    eye = jnp.eye(SSM_BLOCK_GROUPS, dtype=F32)

    def b_block(bp):
        bp = bp.reshape(nblk, SSM_BLOCK_GROUPS, p, SSM_GROUP)
        return jnp.einsum('qgpc,gh->qgchp', bp, eye).reshape(nblk, SSM_BLOCK_CH, SSM_BLOCK_ST)

    def c_block(c):
        c = c.reshape(nblk, SSM_BLOCK_GROUPS, SSM_GROUP, p)
        return jnp.einsum('qgcp,gh->qgphc', c, eye).reshape(nblk, SSM_BLOCK_ST, SSM_BLOCK_CH)

    bblk = jnp.concatenate([b_block(bp_re), b_block(bp_im)], axis=2).astype(BF16)
    cblk = jnp.concatenate([c_block(c_re), -c_block(c_im)], axis=1).astype(BF16)
    return bblk, cblk, d_skip.reshape(1, g * SSM_GROUP), ap_re, ap_im


def _const_spec(shape):
    return pl.BlockSpec(shape, lambda *_: (0,) * len(shape))


def s5_prompt(u, h0r, h0i, params, w_glu, b_glu, chunk):
    t, w = u.shape
    nj = chunk // SUBLANES
    bblk, cblk, d, apr, api = params
    ns = apr.shape[1]
    consts = [h0r, h0i, bblk, cblk, d, apr, api, w_glu, b_glu.reshape(1, w)]
    u_perm = u.reshape(t // chunk, SUBLANES, nj, w).transpose(0, 2, 1, 3).reshape(t, w)
    o_perm, h_re, h_im = pl.pallas_call(
        functools.partial(_s5_prompt_kernel, nj=nj),
        grid=(t // chunk,),
        in_specs=[pl.BlockSpec((chunk, w), lambda i: (i, 0))] + [_const_spec(c.shape) for c in consts],
        out_specs=[pl.BlockSpec((chunk, w), lambda i: (i, 0)), _const_spec((1, ns)), _const_spec((1, ns))],
        out_shape=[jax.ShapeDtypeStruct((t, w), BF16), jax.ShapeDtypeStruct((1, ns), F32),
                   jax.ShapeDtypeStruct((1, ns), F32)],
        scratch_shapes=[pltpu.VMEM((chunk, 2 * SSM_BLOCK_ST), F32), pltpu.VMEM((chunk, w), F32),
                        pltpu.VMEM((1, ns), F32), pltpu.VMEM((1, ns), F32)],
        compiler_params=_cparams("arbitrary"),
        name="s5_prompt",
    )(u_perm, *consts)
    o = o_perm.reshape(t // chunk, nj, SUBLANES, w).transpose(0, 2, 1, 3).reshape(t, w)
    return o, h_re, h_im


def s5_sample(u_tb, h0r, h0i, params, w_glu, b_glu, nb, nt):
    m, w = u_tb.shape
    bblk, cblk, d, apr, api = params
    ns = apr.shape[1]
    consts = [h0r, h0i, bblk, cblk, d, apr, api, w_glu, b_glu.reshape(1, w)]
    return pl.pallas_call(
        functools.partial(_s5_sample_kernel, nb=nb, nt=nt),
        grid=(1,),
        in_specs=[_const_spec(u_tb.shape)] + [_const_spec(c.shape) for c in consts],
        out_specs=[_const_spec((m, w)), _const_spec((nb, ns)), _const_spec((nb, ns))],
        out_shape=[jax.ShapeDtypeStruct((m, w), BF16), jax.ShapeDtypeStruct((nb, ns), F32),
                   jax.ShapeDtypeStruct((nb, ns), F32)],
        scratch_shapes=[pltpu.VMEM((m, 2 * SSM_BLOCK_ST), F32), pltpu.VMEM((m, w), F32)],
        compiler_params=_cparams("arbitrary"),
        name="s5_sample",
    )(u_tb, *consts)


def _log_sigmoid_pair(w):
    lk = jnp.minimum(w, 0.0) - jnp.log(1.0 + jnp.exp(-jnp.abs(w)))
    return lk - w, lk


def _split3_bf16(x):
    hi = x.astype(BF16).astype(F32)
    mid = (x - hi).astype(BF16).astype(F32)
    lo = (x - hi - mid).astype(BF16).astype(F32)
    return hi, mid, lo


def _sb_prompt_kernel(bias_ref, q_ref, k_ref, v_ref, o_ref, acc_ref, carry_ref, w_ref, *, tq, tk, n_hp):
    hp, i = pl.program_id(0), pl.program_id(1)
    kb_per_q = tq // tk
    row = lax.broadcasted_iota(jnp.int32, (tq, tk), 0)
    col = lax.broadcasted_iota(jnp.int32, (tq, tk), 1)
    newer = (lax.broadcasted_iota(jnp.int32, (tk, tk), 0)
             > lax.broadcasted_iota(jnp.int32, (tk, tk), 1)).astype(BF16)
    acc_ref[...] = jnp.zeros_like(acc_ref)
    carry_ref[...] = jnp.zeros_like(carry_ref)

    lane_q = lax.broadcasted_iota(jnp.int32, (tq, SB_HEAD_DIM), 1)
    lane_k = lax.broadcasted_iota(jnp.int32, (tk, SB_HEAD_DIM), 1)
    q_units = jnp.where(lane_q < 3, 1.0, 0.0).astype(BF16)
    q_aug, k_bias = [], []
    for hh in range(n_hp):
        hi, mid, lo = _split3_bf16(jnp.full((tk, SB_HEAD_DIM), -bias_ref[hp * n_hp + hh], F32))
        k_bias.append(jnp.where(lane_k == 0, hi, jnp.where(lane_k == 1, mid, jnp.where(lane_k == 2, lo, 0.0)))
                      .astype(BF16))
        q_aug.append(jnp.concatenate([q_ref[:, hh * SB_HEAD_DIM:(hh + 1) * SB_HEAD_DIM], q_units], axis=1))

    def logits(kb, hh):
        rows = pl.ds(pl.multiple_of(kb * tk, tk), tk)
        k_aug = jnp.concatenate([k_ref[rows, hh * SB_HEAD_DIM:(hh + 1) * SB_HEAD_DIM], k_bias[hh]], axis=1)
        return lax.dot_general(q_aug[hh], k_aug, (((1,), (1,)), ((), ())), preferred_element_type=F32)

    def weigh(kb, hh, w, mask):
        rows = pl.ds(pl.multiple_of(kb * tk, tk), tk)
        ls_s, lk_s, lkb_s = [], [], []
        for s in range(tq // STRIP):
            r = slice(s * STRIP, (s + 1) * STRIP)
            ls, lk = _log_sigmoid_pair(w[r])
            if mask is not None:
                lk = jnp.where(mask[r], lk, 0.0)
            ls_s.append(ls)
            lk_s.append(jnp.sum(lk, axis=1, keepdims=True))
            lkb_s.append(lk.astype(BF16))
        cum = jnp.dot(jnp.concatenate(lkb_s, axis=0), newer, preferred_element_type=F32)
        a_s = []
        for s in range(tq // STRIP):
            r = slice(s * STRIP, (s + 1) * STRIP)
            a = jnp.exp(ls_s[s] + cum[r] + carry_ref[hh, r, :])
            if mask is not None:
                a = jnp.where(mask[r], a, 0.0)
            a_s.append(a.astype(BF16))
        acc_ref[hh] += jnp.dot(jnp.concatenate(a_s, axis=0), v_ref[rows, hh * SB_HEAD_DIM:(hh + 1) * SB_HEAD_DIM],
                               preferred_element_type=F32)
        carry_ref[hh] += jnp.concatenate(lk_s, axis=0)

    def tile(kb, mask):
        for hh in range(n_hp):
            weigh(kb, hh, logits(kb, hh), mask)

    for d in reversed(range(kb_per_q)):
        tile(i * kb_per_q + d, col + d * tk < row)

    for hh in range(n_hp):
        w_ref[hh] = logits(jnp.maximum(i * kb_per_q - 1, 0), hh)

    def body(jj, _):
        base = (i - jj) * kb_per_q
        for hh in range(n_hp):
            weigh(base - 1, hh, w_ref[hh], None)
        for d in range(1, kb_per_q):
            tile(base - 1 - d, None)
        for hh in range(n_hp):
            w_ref[hh] = logits(jnp.maximum(base - kb_per_q - 1, 0), hh)
        return 0

    lax.fori_loop(0, i, body, 0)
    for hh in range(n_hp):
        o_ref[:, hh * SB_HEAD_DIM:(hh + 1) * SB_HEAD_DIM] = acc_ref[hh].astype(o_ref.dtype)


def sb_attention_prompt(q, k, v, bias, tq=512, tk=256, heads_per_step=2):
    t, w = q.shape
    n_heads = w // SB_HEAD_DIM
    tq, tk, n_hp = _tile(t, tq), _tile(t, tk), heads_per_step
    assert tq % tk == 0 and n_heads % n_hp == 0
    wb = n_hp * SB_HEAD_DIM
    grid_spec = pltpu.PrefetchScalarGridSpec(
        num_scalar_prefetch=1,
        grid=(n_heads // n_hp, t // tq),
        in_specs=[pl.BlockSpec((tq, wb), lambda h, i, b: (i, h)),
                  pl.BlockSpec((t, wb), lambda h, i, b: (0, h)),
                  pl.BlockSpec((t, wb), lambda h, i, b: (0, h))],
        out_specs=pl.BlockSpec((tq, wb), lambda h, i, b: (i, h)),
        scratch_shapes=[pltpu.VMEM((n_hp, tq, SB_HEAD_DIM), F32), pltpu.VMEM((n_hp, tq, 1), F32),
                        pltpu.VMEM((n_hp, tq, tk), F32)],
    )
    return pl.pallas_call(
        functools.partial(_sb_prompt_kernel, tq=tq, tk=tk, n_hp=n_hp),
        grid_spec=grid_spec,
        out_shape=jax.ShapeDtypeStruct((t, w), BF16),
        compiler_params=_cparams("parallel", "arbitrary"),
        name="sb_attention_prompt",
    )(bias, q, k, v)


def _sb_sample_kernel(pt_ref, *refs, n_pp, n_heads):
    wq_ref, bias_ref, kn_ref, vn_ref = refs[:4]
    kp_refs = refs[4:4 + n_pp]
    vp_refs = refs[4 + n_pp:4 + 2 * n_pp]
    o_ref, acc_ref, carry_ref = refs[4 + 2 * n_pp:]
    p = pl.program_id(1)
    tk = PAGE_SIZE
    nq = o_ref.shape[0]
    dh = SB_HEAD_DIM
    newer = (lax.broadcasted_iota(jnp.int32, (tk, tk), 1)
             > lax.broadcasted_iota(jnp.int32, (tk, tk), 0)).astype(BF16)

    def head_rows(ref, h):
        return ref[pl.ds(h, tk, stride=n_heads), :]

    def weights(z, mask, carry):
        ls, lk = _log_sigmoid_pair(z)
        if mask is not None:
            lk = jnp.where(mask, lk, 0.0)
        after = jnp.dot(newer, lk.astype(BF16), preferred_element_type=F32) + carry
        a = jnp.exp(ls + after)
        if mask is not None:
            a = jnp.where(mask, a, 0.0)
        return a.T, jnp.sum(lk, axis=0, keepdims=True)

    @pl.when(p == 0)
    def _():
        lane_q = lax.broadcasted_iota(jnp.int32, (tk, LANES), 1) % nq
        key = lax.broadcasted_iota(jnp.int32, (tk, LANES), 0)
        z = jnp.dot(kn_ref[...], wq_ref[...], preferred_element_type=F32) - bias_ref[...]
        at, tot = weights(z, key < lane_q, jnp.zeros((1, LANES), F32))
        carry_ref[...] = tot
        for h in range(n_heads):
            acc_ref[h * nq:(h + 1) * nq, :] = jnp.dot(at[h * nq:(h + 1) * nq, :].astype(BF16),
                                                      vn_ref[:, h * dh:(h + 1) * dh], preferred_element_type=F32)

    z = -bias_ref[...]
    for hp in range(n_heads // 2):
        lhs = jnp.concatenate(
            [jnp.concatenate([head_rows(kp_refs[r], 2 * hp), head_rows(kp_refs[r], 2 * hp + 1)], axis=1)
             for r in range(n_pp)], axis=0).astype(BF16)
        z = z + jnp.dot(lhs, wq_ref[2 * hp * dh:(2 * hp + 2) * dh, :], preferred_element_type=F32)
    carry = carry_ref[...]
    ats = []
    for r in range(n_pp):
        at, tot = weights(z[r * tk:(r + 1) * tk, :], None, carry)
        carry = carry + tot
        ats.append(at)
    carry_ref[...] = carry
    for h in range(n_heads):
        lhs = jnp.concatenate([at[h * nq:(h + 1) * nq, :] for at in ats], axis=1).astype(BF16)
        rhs = jnp.concatenate([head_rows(vp_refs[r], h) for r in range(n_pp)], axis=0).astype(BF16)
        acc_ref[h * nq:(h + 1) * nq, :] += jnp.dot(lhs, rhs, preferred_element_type=F32)

    @pl.when(p == pl.num_programs(1) - 1)
    def _():
        for h in range(n_heads):
            o_ref[:, h * dh:(h + 1) * dh] = acc_ref[h * nq:(h + 1) * nq, :]


def sb_attention_sample(q, k_new, v_new, bias, cache_k, cache_v, page_table, pages_per_step=8):
    nb, nq, w = q.shape
    n_heads = w // SB_HEAD_DIM
    n_pages = page_table.shape[1]
    n_pp = pages_per_step
    assert n_pages % n_pp == 0 and n_heads * nq <= LANES and nq <= PAGE_SIZE and n_heads % 2 == 0
    q4 = q.reshape(nb, nq, n_heads, SB_HEAD_DIM).transpose(0, 2, 3, 1)
    wq = jnp.einsum('bhdt,hg->bhdgt', q4, jnp.eye(n_heads, dtype=q.dtype)).reshape(nb, w, n_heads * nq)
    wq = jnp.pad(wq, ((0, 0), (0, 0), (0, LANES - n_heads * nq)))
    bias_row = jnp.pad(jnp.repeat(bias, nq), (0, LANES - n_heads * nq)).reshape(1, LANES)
    pad = ((0, 0), (0, PAGE_SIZE - nq), (0, 0))
    k_new, v_new = jnp.pad(k_new, pad), jnp.pad(v_new, pad)

    page_rows = PAGE_SIZE * n_heads

    def page_spec(r):
        return pl.BlockSpec((None, page_rows, SB_HEAD_DIM),
                            lambda b, p, pt: (pt[b, n_pages - 1 - (p * n_pp + r)], 0, 0))

    per_b = lambda b, p, pt: (b, 0, 0)
    grid_spec = pltpu.PrefetchScalarGridSpec(
        num_scalar_prefetch=1,
        grid=(nb, n_pages // n_pp),
        in_specs=[pl.BlockSpec((None, w, LANES), per_b), pl.BlockSpec((1, LANES), lambda b, p, pt: (0, 0)),
                  pl.BlockSpec((None, PAGE_SIZE, w), per_b), pl.BlockSpec((None, PAGE_SIZE, w), per_b)]
                 + [page_spec(r) for r in range(n_pp)] + [page_spec(r) for r in range(n_pp)],
        out_specs=pl.BlockSpec((None, nq, w), per_b),
        scratch_shapes=[pltpu.VMEM((n_heads * nq, SB_HEAD_DIM), F32), pltpu.VMEM((1, LANES), F32)],
    )
    return pl.pallas_call(
        functools.partial(_sb_sample_kernel, n_pp=n_pp, n_heads=n_heads),
        grid_spec=grid_spec,
        out_shape=jax.ShapeDtypeStruct((nb, nq, w), F32),
        compiler_params=_cparams("parallel", "arbitrary"),
        name="sb_attention_sample",
    )(page_table, wq, bias_row, k_new, v_new, *([cache_k] * n_pp), *([cache_v] * n_pp))


def _mem_attn_kernel(q_ref, k_ref, v_ref, o_ref, *, scale):
    dh = o_ref.shape[1] // MEM_HEADS
    for h in range(MEM_HEADS):
        cols = slice(h * dh, (h + 1) * dh)
        kh = k_ref[:, cols].astype(BF16)
        vh = v_ref[:, cols].astype(BF16)
        s = lax.dot_general(q_ref[:, cols], kh, (((1,), (1,)), ((), ())), preferred_element_type=F32) * scale
        e = jnp.exp(s - jnp.max(s, axis=-1, keepdims=True))
        p = e / jnp.sum(e, axis=-1, keepdims=True)
        o_ref[:, cols] = jnp.dot(p.astype(BF16), vh, preferred_element_type=F32).astype(o_ref.dtype)


def memory_attention(q, mem_k, mem_v, tq=512):
    nb, t, w = q.shape
    n_mem = mem_k.shape[1]
    tq = _tile(t, tq)
    return pl.pallas_call(
        functools.partial(_mem_attn_kernel, scale=(w // MEM_HEADS) ** -0.5),
        grid=(nb, t // tq),
        in_specs=[pl.BlockSpec((None, tq, w), lambda b, i: (b, i, 0)),
                  pl.BlockSpec((None, n_mem, w), lambda b, i: (b, 0, 0)),
                  pl.BlockSpec((None, n_mem, w), lambda b, i: (b, 0, 0))],
        out_specs=pl.BlockSpec((None, tq, w), lambda b, i: (b, i, 0)),
        out_shape=jax.ShapeDtypeStruct((nb, t, w), BF16),
        compiler_params=_cparams("parallel", "arbitrary"),
        name="memory_attention",
    )(q, mem_k, mem_v)


def _merge_kernel(hn_ref, o0_ref, o1_ref, o2_ref, g0_ref, g1_ref, g2_ref, b0_ref, b1_ref, b2_ref, out_ref):
    hn = hn_ref[...]
    total = None
    for o_ref, g_ref, b_ref in ((o0_ref, g0_ref, b0_ref), (o1_ref, g1_ref, b1_ref), (o2_ref, g2_ref, b2_ref)):
        gate = _sigmoid(jnp.dot(hn, g_ref[...], preferred_element_type=F32))
        term = gate * jnp.dot(o_ref[...], b_ref[...], preferred_element_type=F32)
        total = term if total is None else total + term
    out_ref[...] = total.astype(out_ref.dtype)


def gated_merge(hn, branches, w_gates, w_branches, tm=512, tn=512):
    m, d = hn.shape
    tm, tn = _tile(m, tm), _tile(d, tn)
    nj = d // tn
    row = lambda i, j: (i, 0)
    in_specs = [pl.BlockSpec((tm, d), row)]
    in_specs += [pl.BlockSpec((tm, o.shape[1]), row) for o in branches]
    in_specs += [pl.BlockSpec((d, tn), functools.partial(lambda i, j, b: (0, b * nj + j), b=b)) for b in range(3)]
    in_specs += [pl.BlockSpec((w.shape[0], tn), lambda i, j: (0, j)) for w in w_branches]
    return pl.pallas_call(
        _merge_kernel,
        grid=(m // tm, nj),
        in_specs=in_specs,
        out_specs=pl.BlockSpec((tm, tn), lambda i, j: (i, j)),
        out_shape=jax.ShapeDtypeStruct((m, d), BF16),
        compiler_params=_cparams("parallel", "arbitrary"),
        name="gated_merge",
    )(hn, *branches, w_gates, w_gates, w_gates, *w_branches)


MOE_TILE = 1024
MOE_ROWS = 256
MOE_EXPERTS_PER_STEP = 2
ROUTER_GROUP_LANE = 64
ROUTER_RANK_LANE = 65


def _router_kernel(x_ref, g_ref, wr_ref, br_ref, hn_ref, gate_ref, cnt_ref, run_ref):
    hn = _rms(x_ref[...], g_ref[...])
    hn_ref[...] = hn.astype(hn_ref.dtype)
    logits = jnp.dot(hn, wr_ref[...], preferred_element_type=F32, precision=lax.Precision.HIGHEST) + br_ref[...]
    lane = lax.broadcasted_iota(jnp.int32, logits.shape, 1).astype(F32)
    neg, far = -1e30, 1e9
    rmax = lambda v: jnp.max(v, axis=-1, keepdims=True)
    rmin = lambda v: jnp.min(v, axis=-1, keepdims=True)
    rsum = lambda v: jnp.sum(v, axis=-1, keepdims=True)
    is_group = lane < N_GROUPS
    gl = jnp.where(is_group, logits, neg)
    gmax = rmax(gl)
    g_val = 1.0 / rsum(jnp.where(is_group, jnp.exp(gl - gmax), 0.0))
    g_idx = rmin(jnp.where(gl == gmax, lane, far))
    lo = ROUTER_EXPERT_LANE0 + EXPERTS_PER_GROUP * g_idx
    in_group = jnp.logical_and(lane >= lo, lane < lo + EXPERTS_PER_GROUP)
    el = jnp.where(in_group, logits, neg)
    ex = jnp.where(in_group, jnp.exp(el - rmax(el)), 0.0)
    prob = jnp.where(in_group, ex / rsum(ex), -1.0)
    v1 = rmax(prob)
    i1 = rmin(jnp.where(prob == v1, lane, far))
    rest = jnp.where(lane == i1, -1.0, prob)
    v2 = rmax(rest)
    i2 = rmin(jnp.where(rest == v2, lane, far))
    within = jnp.where(lane == i1, v1 / (v1 + v2), jnp.where(lane == i2, v2 / (v1 + v2), 0.0))

    @pl.when(pl.program_id(1) == 0)
    def _():
        run_ref[...] = jnp.zeros_like(run_ref)

    tm = logits.shape[0]
    chose = jnp.where(lane == g_idx, 1.0, 0.0)
    earlier = (lax.broadcasted_iota(jnp.int32, (tm, tm), 1)
               < lax.broadcasted_iota(jnp.int32, (tm, tm), 0)).astype(BF16)
    before = jnp.dot(earlier, chose.astype(BF16), preferred_element_type=F32) + run_ref[...]
    rank = rsum(chose * before)
    run_ref[...] += jnp.sum(chose, axis=0, keepdims=True)
    cnt_ref[...] = run_ref[...]
    gate_ref[...] = (g_val * within + jnp.where(lane == ROUTER_GROUP_LANE, g_idx, 0.0)
                     + jnp.where(lane == ROUTER_RANK_LANE, rank, 0.0))


def moe_router(x, g, w_router, b_router, tile, tm=256):
    m, d = x.shape
    tm = _tile(tile, tm)
    sub = tile // tm
    row = lambda i, s: (i * sub + s, 0)
    return pl.pallas_call(
        _router_kernel,
        grid=(m // tile, sub),
        in_specs=[pl.BlockSpec((tm, d), row), _const_spec((1, d)), _const_spec((d, LANES)), _const_spec((1, LANES))],
        out_specs=[pl.BlockSpec((tm, d), row), pl.BlockSpec((tm, LANES), row),
                   pl.BlockSpec((None, 1, LANES), lambda i, s: (i, 0, 0))],
        out_shape=[jax.ShapeDtypeStruct((m, d), BF16), jax.ShapeDtypeStruct((m, LANES), F32),
                   jax.ShapeDtypeStruct((m // tile, 1, LANES), F32)],
        scratch_shapes=[pltpu.VMEM((1, LANES), F32)],
        compiler_params=_cparams("parallel", "arbitrary"),
        name="moe_router",
    )(x, g.reshape(1, d), w_router, b_router)


def _experts_kernel(bg_ref, nb_ref, hn_ref, gate_ref, x_ref, drow_ref, dcol_ref, wgu_ref, wd_ref, gf_ref, o_ref,
                    xs_ref, gs_ref, yb_ref):
    i, b, j = pl.program_id(0), pl.program_id(1), pl.program_id(2)
    tile = hn_ref.shape[0]
    rows = xs_ref.shape[0]
    ff = wd_ref.shape[1]
    active = b < nb_ref[i]
    last_j = j == pl.num_programs(2) - 1

    @pl.when(jnp.logical_and(b == 0, j == 0))
    def _():
        o_ref[...] = x_ref[...]

    @pl.when(jnp.logical_and(active, j == 0))
    def _():
        slot = lax.broadcasted_iota(jnp.int32, (rows, tile), 0) + b * rows
        pick = jnp.where(drow_ref[...] == slot, 1.0, 0.0).astype(BF16)
        xs_ref[...] = jnp.dot(pick, hn_ref[...], preferred_element_type=F32).astype(BF16)
        gs = None
        for part in _split3_bf16(gate_ref[...]):
            term = jnp.dot(pick, part.astype(BF16), preferred_element_type=F32)
            gs = term if gs is None else gs + term
        gs_ref[...] = gs
        yb_ref[...] = jnp.zeros_like(yb_ref)

    @pl.when(active)
    def _():
        lane = lax.broadcasted_iota(jnp.int32, gs_ref.shape, 1)
        y = yb_ref[...]
        for k in range(MOE_EXPERTS_PER_STEP):
            e_lane = ROUTER_EXPERT_LANE0 + bg_ref[i, b] * EXPERTS_PER_GROUP + j * MOE_EXPERTS_PER_STEP + k
            gate = jnp.sum(jnp.where(lane == e_lane, gs_ref[...], 0.0), axis=-1, keepdims=True)
            au = jnp.dot(xs_ref[...], wgu_ref[k], preferred_element_type=F32)
            a, up = au[:, :ff], au[:, ff:]
            act = a * _sigmoid(a) * up * gate
            y = y + jnp.dot(act.astype(BF16), wd_ref[k], preferred_element_type=F32)
        yb_ref[...] = y

    @pl.when(jnp.logical_and(active, last_j))
    def _():
        slot = lax.broadcasted_iota(jnp.int32, (tile, rows), 1) + b * rows
        put = jnp.where(dcol_ref[...] == slot, 1.0, 0.0).astype(BF16)
        o_ref[...] += jnp.dot(put, yb_ref[...].astype(BF16), preferred_element_type=F32)

    @pl.when(jnp.logical_and(b == pl.num_programs(1) - 1, last_j))
    def _():
        o_ref[...] = _rms(o_ref[...], gf_ref[...])


def moe_experts_final_norm(hn, route, counts, x, w_gate_up, w_down, g_final, tile):
    m, d = x.shape
    n_exp, _, ff2 = w_gate_up.shape
    n_tiles = m // tile
    rows = min(MOE_ROWS, tile)
    n_blocks = tile // rows + N_GROUPS - 1
    per = MOE_EXPERTS_PER_STEP
    steps = EXPERTS_PER_GROUP // per
    w_gate_up = w_gate_up.reshape(n_exp // per, per, d, ff2)
    w_down = w_down.reshape(n_exp // per, per, ff2 // 2, d)
    cnt = counts[:, 0, :N_GROUPS].astype(jnp.int32)
    blocks_g = (cnt + rows - 1) // rows
    end_g = jnp.cumsum(blocks_g, axis=1)
    start_g = end_g - blocks_g
    n_blk = end_g[:, -1]
    bidx = jnp.arange(n_blocks, dtype=jnp.int32)[None, :]
    blk_group = jnp.sum(bidx[:, :, None] >= end_g[:, None, :], axis=2).astype(jnp.int32)
    last_group = jnp.sum(jnp.where(bidx == jnp.maximum(n_blk - 1, 0)[:, None], blk_group, 0), axis=1, keepdims=True)
    blk_group = jnp.minimum(blk_group, last_group)
    grp = route[:, ROUTER_GROUP_LANE].astype(jnp.int32).reshape(n_tiles, tile)
    rank = route[:, ROUTER_RANK_LANE].astype(jnp.int32).reshape(n_tiles, tile)
    start_t = sum(jnp.where(grp == g, start_g[:, g:g + 1], 0) for g in range(N_GROUPS))
    dest = rows * start_t + rank

    def w_index(i, b, j, bg, nb):
        live = b < nb[i]
        last = jnp.maximum(nb[i] - 1, 0)
        return (jnp.where(live, bg[i, b] * steps + j, bg[i, last] * steps + steps - 1), 0, 0, 0)

    row = lambda i, b, j, bg, nb: (i, 0)
    grid_spec = pltpu.PrefetchScalarGridSpec(
        num_scalar_prefetch=2,
        grid=(n_tiles, n_blocks, steps),
        in_specs=[pl.BlockSpec((tile, d), row, pipeline_mode=pl.Buffered(1)), pl.BlockSpec((tile, LANES), row),
                  pl.BlockSpec((tile, d), row, pipeline_mode=pl.Buffered(1)),
                  pl.BlockSpec((None, 1, tile), lambda i, b, j, bg, nb: (i, 0, 0)),
                  pl.BlockSpec((tile, 1), row),
                  pl.BlockSpec((None, per, d, ff2), w_index), pl.BlockSpec((None, per, ff2 // 2, d), w_index),
                  pl.BlockSpec((1, d), lambda i, b, j, bg, nb: (0, 0))],
        out_specs=pl.BlockSpec((tile, d), row),
        scratch_shapes=[pltpu.VMEM((rows, d), BF16), pltpu.VMEM((rows, LANES), F32), pltpu.VMEM((rows, d), F32)],
    )
    return pl.pallas_call(
        _experts_kernel,
        grid_spec=grid_spec,
        out_shape=jax.ShapeDtypeStruct((m, d), F32),
        compiler_params=_cparams("parallel", "arbitrary", "arbitrary"),
        name="moe_experts",
    )(blk_group, n_blk, hn, route, x, dest.reshape(n_tiles, 1, tile), dest.reshape(m, 1), w_gate_up, w_down,
      g_final.reshape(1, d))


S5_CHUNK = 512


def kernel(x_prompt, x_sample, mem_prompt, cache_k, cache_v, page_table, cache_mem_k, cache_mem_v, state_ssm_re, state_ssm_im, g_mix_norm, w_in, sb_logit_bias, ssm_lambda_re, ssm_lambda_im, ssm_log_step, ssm_b_re, ssm_b_im, ssm_c_re, ssm_c_im, ssm_d, w_glu, b_glu, g_mem_norm, w_mem_k, w_mem_v, w_branch_ssm, w_branch_sb, w_branch_mem, w_out, g_ffn_norm, w_group_router, b_group_router, w_expert_router, b_expert_router, w_exp_gate, w_exp_up, w_exp_down, g_final):
    depth = w_in.shape[0]
    assert depth == 1, "one mixer + MoE layer per step"
    bp, tp, d = x_prompt.shape
    bs, ts, _ = x_sample.shape
    assert bp == 1
    n_groups, n_state = ssm_lambda_re.shape[1:]
    ssm_w = n_groups * SSM_GROUP
    n_st = n_groups * n_state
    sb_w = cache_k.shape[3] * cache_k.shape[4]
    mem_w = cache_mem_k.shape[3] * cache_mem_k.shape[4]
    n_mem = mem_prompt.shape[1]
    assert ssm_w == sb_w == mem_w
    n_heads = sb_w // SB_HEAD_DIM
    mem_dh = mem_w // MEM_HEADS
    l = 0

    w5 = w_in[l, :, :5 * ssm_w].astype(BF16)
    w_gates = w_in[l, :, 5 * ssm_w:].astype(BF16)
    w_branches = [w_branch_ssm[l].astype(BF16), w_branch_sb[l].astype(BF16), w_branch_mem[l].astype(BF16)]
    w_out_b = w_out[l].astype(BF16)
    w_glu_b = w_glu[l].astype(BF16)
    w_memkv = jnp.concatenate([w_mem_k[l], w_mem_v[l]], axis=1).astype(BF16)
    w_gate_up = jnp.concatenate([w_exp_gate[l], w_exp_up[l]], axis=2).astype(BF16)
    w_down = w_exp_down[l].astype(BF16)
    w_router = jnp.zeros((d, LANES), F32)
    w_router = w_router.at[:, :N_GROUPS].set(w_group_router[l])
    w_router = w_router.at[:, ROUTER_EXPERT_LANE0:ROUTER_EXPERT_LANE0 + N_EXPERTS].set(w_expert_router[l])
    b_router = jnp.zeros((1, LANES), F32)
    b_router = b_router.at[0, :N_GROUPS].set(b_group_router[l])
    b_router = b_router.at[0, ROUTER_EXPERT_LANE0:ROUTER_EXPERT_LANE0 + N_EXPERTS].set(b_expert_router[l])
    s5p = _s5_params(ssm_lambda_re[l], ssm_lambda_im[l], ssm_log_step[l], ssm_b_re[l], ssm_b_im[l],
                     ssm_c_re[l], ssm_c_im[l], ssm_d[l], S5_CHUNK // SUBLANES)
    bias = sb_logit_bias[l]

    mn = rmsnorm(mem_prompt.reshape(n_mem, d), g_mem_norm[l], BF16)
    (mem_kv,) = matmul(mn, w_memkv, (F32,), tm=n_mem, tn=1024, name="memory_kv")
    mk_p, mv_p = mem_kv[:, :mem_w], mem_kv[:, mem_w:]

    def mixer_and_moe(x, attend, mem_k, mem_v, s5):
        hn = rmsnorm(x, g_mix_norm[l], BF16)
        u, q, k, kb, v, vb, qm = in_projection(hn, w5, ssm_w)
        o_ssm, h_re, h_im = s5(u)
        o_sb = attend(q, kb, vb)
        o_mem = memory_attention(qm.reshape(mem_k.shape[0], -1, mem_w), mem_k, mem_v).reshape(-1, mem_w)
        merged = gated_merge(hn, (o_ssm, o_sb, o_mem), w_gates, w_branches)
        (x2,) = matmul(merged, w_out_b, (F32,), res=x, tm=512, tn=1024, name="out_projection")
        moe_tile = min(MOE_TILE, x2.shape[0])
        hn2, route, counts = moe_router(x2, g_ffn_norm[l], w_router, b_router, moe_tile)
        y = moe_experts_final_norm(hn2, route, counts, x2, w_gate_up, w_down, g_final, moe_tile)
        return y, k, v, h_re, h_im

    zeros = jnp.zeros((1, n_st), F32)
    y_p, k_p, v_p, hr_p, hi_p = mixer_and_moe(
        x_prompt.reshape(tp, d),
        lambda q, kb, vb: sb_attention_prompt(q, kb, vb, bias),
        mk_p.reshape(1, n_mem, mem_w), mv_p.reshape(1, n_mem, mem_w),
        lambda u: s5_prompt(u, zeros, zeros, s5p, w_glu_b, b_glu[l], S5_CHUNK))

    def s5_s(u):
        u_tb = u.reshape(bs, ts, ssm_w).transpose(1, 0, 2).reshape(ts * bs, ssm_w)
        o_tb, h_re, h_im = s5_sample(u_tb, state_ssm_re[l].reshape(bs, n_st), state_ssm_im[l].reshape(bs, n_st),
                                     s5p, w_glu_b, b_glu[l], bs, ts)
        return o_tb.reshape(ts, bs, ssm_w).transpose(1, 0, 2).reshape(bs * ts, ssm_w), h_re, h_im

    def attend_s(q, kb, vb):
        o = sb_attention_sample(q.reshape(bs, ts, sb_w), kb.reshape(bs, ts, sb_w), vb.reshape(bs, ts, sb_w), bias,
                                cache_k.reshape(-1, PAGE_SIZE * n_heads, SB_HEAD_DIM),
                                cache_v.reshape(-1, PAGE_SIZE * n_heads, SB_HEAD_DIM),
                                page_table + l * cache_k.shape[1])
        return o.reshape(bs * ts, sb_w).astype(BF16)

    y_s, k_s, v_s, hr_s, hi_s = mixer_and_moe(
        x_sample.reshape(bs * ts, d), attend_s,
        cache_mem_k[l].reshape(bs, n_mem, mem_w), cache_mem_v[l].reshape(bs, n_mem, mem_w), s5_s)

    return (y_p.reshape(bp, tp, d), y_s.reshape(bs, ts, d),
            k_p.reshape(1, bp, tp, n_heads, SB_HEAD_DIM), v_p.reshape(1, bp, tp, n_heads, SB_HEAD_DIM),
            k_s.reshape(1, bs, ts, n_heads, SB_HEAD_DIM), v_s.reshape(1, bs, ts, n_heads, SB_HEAD_DIM),
            hr_p.reshape(1, bp, n_groups, n_state), hi_p.reshape(1, bp, n_groups, n_state),
            hr_s.reshape(1, bs, n_groups, n_state), hi_s.reshape(1, bs, n_groups, n_state),
            mk_p.reshape(1, bp, n_mem, MEM_HEADS, mem_dh), mv_p.reshape(1, bp, n_mem, MEM_HEADS, mem_dh))
```

```python
import functools
import math

import jax
import jax.numpy as jnp
from jax import lax
from jax.experimental import pallas as pl
from jax.experimental.pallas import tpu as pltpu

F32 = jnp.float32
BF16 = jnp.bfloat16

V7X_VMEM_BYTES = 64 * 1024 * 1024
VMEM_LIMIT = V7X_VMEM_BYTES - 8 * 1024 * 1024
LANES = 128
SUBLANES = 8

SSM_GROUP = 16
SSM_STATE = 64
SB_HEAD_DIM = 128
MEM_HEADS = 4
PAGE_SIZE = 128
N_GROUPS = 4
EXPERTS_PER_GROUP = 8
N_EXPERTS = N_GROUPS * EXPERTS_PER_GROUP
RMS_EPS = 1e-6

SSM_BLOCK_GROUPS = 16
SSM_BLOCK_CH = SSM_BLOCK_GROUPS * SSM_GROUP
SSM_BLOCK_ST = SSM_BLOCK_GROUPS * SSM_STATE
SCAN_LANES = 512
STRIP = 64
SAMPLE_SPLITS = 1
ROUTER_EXPERT_LANE0 = 32


def _cparams(*sem):
    return pltpu.CompilerParams(dimension_semantics=sem, vmem_limit_bytes=VMEM_LIMIT)


def _sigmoid(x):
    return 1.0 / (1.0 + jnp.exp(-x))


def _tile(m, pref):
    t = min(m, pref)
    assert m % t == 0, (m, pref)
    return t


def _rms(x, g):
    return x * lax.rsqrt(jnp.mean(x * x, axis=-1, keepdims=True) + RMS_EPS) * g


def _rms_kernel(x_ref, g_ref, o_ref):
    o_ref[...] = _rms(x_ref[...], g_ref[...]).astype(o_ref.dtype)


def rmsnorm(x, g, out_dtype, tm=512):
    m, d = x.shape
    tm = _tile(m, tm)
    return pl.pallas_call(
        _rms_kernel,
        grid=(m // tm,),
        in_specs=[pl.BlockSpec((tm, d), lambda i: (i, 0)), pl.BlockSpec((1, d), lambda i: (0, 0))],
        out_specs=pl.BlockSpec((tm, d), lambda i: (i, 0)),
        out_shape=jax.ShapeDtypeStruct((m, d), out_dtype),
        compiler_params=_cparams("parallel"),
        name="rmsnorm",
    )(x, g.reshape(1, d))


def _mm_kernel(*refs, has_res):
    a_ref, b_ref = refs[0], refs[1]
    o_refs = refs[3:] if has_res else refs[2:]
    acc = jnp.dot(a_ref[...], b_ref[...], preferred_element_type=F32)
    if has_res:
        acc = refs[2][...] + acc
    for o_ref in o_refs:
        o_ref[...] = acc.astype(o_ref.dtype)


def matmul(a, b, out_dtypes, res=None, tm=512, tn=1024, name="matmul"):
    m, k = a.shape
    n = b.shape[1]
    tm, tn = _tile(m, tm), _tile(n, tn)
    in_specs = [pl.BlockSpec((tm, k), lambda i, j: (i, 0)), pl.BlockSpec((k, tn), lambda i, j: (0, j))]
    args = [a, b]
    if res is not None:
        in_specs.append(pl.BlockSpec((tm, tn), lambda i, j: (i, j)))
        args.append(res)
    outs = pl.pallas_call(
        functools.partial(_mm_kernel, has_res=res is not None),
        grid=(m // tm, n // tn),
        in_specs=in_specs,
        out_specs=[pl.BlockSpec((tm, tn), lambda i, j: (i, j)) for _ in out_dtypes],
        out_shape=[jax.ShapeDtypeStruct((m, n), dt) for dt in out_dtypes],
        compiler_params=_cparams("parallel", "arbitrary"),
        name=name,
    )(*args)
    return outs


def _proj_kernel(a_ref, b_ref, u_ref, q_ref, k_ref, kb_ref, v_ref, vb_ref, qm_ref, *, q_scale):
    j = pl.program_id(1)
    acc = jnp.dot(a_ref[...], b_ref[...], preferred_element_type=F32)

    @pl.when(j == 0)
    def _():
        u_ref[...] = acc

    @pl.when(j == 1)
    def _():
        q_ref[...] = (acc * q_scale).astype(BF16)

    @pl.when(j == 2)
    def _():
        k_ref[...] = acc
        kb_ref[...] = acc.astype(BF16)

    @pl.when(j == 3)
    def _():
        v_ref[...] = acc
        vb_ref[...] = acc.astype(BF16)

    @pl.when(j == 4)
    def _():
        qm_ref[...] = acc.astype(BF16)


def in_projection(hn, w5, width, tm=512):
    m, k = hn.shape
    tm = _tile(m, tm)
    dts = (F32, BF16, F32, BF16, F32, BF16, BF16)
    return pl.pallas_call(
        functools.partial(_proj_kernel, q_scale=-(SB_HEAD_DIM ** -0.5)),
        grid=(m // tm, 5),
        in_specs=[pl.BlockSpec((tm, k), lambda i, j: (i, 0)), pl.BlockSpec((k, width), lambda i, j: (0, j))],
        out_specs=[pl.BlockSpec((tm, width), lambda i, j: (i, 0)) for _ in dts],
        out_shape=[jax.ShapeDtypeStruct((m, width), dt) for dt in dts],
        compiler_params=_cparams("parallel", "arbitrary"),
        name="in_projection",
    )(hn, w5)


def _cmul_add(ar, ai, hr, hi, xr, xi):
    return ar * hr - ai * hi + xr, ar * hi + ai * hr + xi


def _s5_readout(q, x_ref, u_ref, cblk_ref, d_ref, z_ref):
    cols = slice(q * SSM_BLOCK_CH, (q + 1) * SSM_BLOCK_CH)
    y = jnp.dot(x_ref[...].astype(BF16), cblk_ref[q], preferred_element_type=F32)
    y = y + d_ref[:, cols] * u_ref[:, cols]
    z_ref[:, cols] = jax.nn.gelu(y)


def _s5_glu(z_ref, wglu_ref, bglu_ref):
    z = z_ref[...]
    gate = _sigmoid(jnp.dot(z.astype(BF16), wglu_ref[...], preferred_element_type=F32) + bglu_ref[...])
    return z * gate


def _s5_prompt_kernel(u_ref, h0r_ref, h0i_ref, bblk_ref, cblk_ref, d_ref, apr_ref, api_ref, wglu_ref, bglu_ref,
                      o_ref, hr_out_ref, hi_out_ref, x_ref, z_ref, hcr_ref, hci_ref, *, nj):
    n_blocks = bblk_ref.shape[0]

    @pl.when(pl.program_id(0) == 0)
    def _():
        hcr_ref[...] = h0r_ref[...]
        hci_ref[...] = h0i_ref[...]

    for q in range(n_blocks):
        ub = u_ref[:, q * SSM_BLOCK_CH:(q + 1) * SSM_BLOCK_CH]
        x_ref[...] = jnp.dot(ub.astype(BF16), bblk_ref[q], preferred_element_type=F32)
        for c in range(SSM_BLOCK_ST // SCAN_LANES):
            re = slice(c * SCAN_LANES, (c + 1) * SCAN_LANES)
            im = slice(SSM_BLOCK_ST + c * SCAN_LANES, SSM_BLOCK_ST + (c + 1) * SCAN_LANES)
            st = slice(q * SSM_BLOCK_ST + c * SCAN_LANES, q * SSM_BLOCK_ST + (c + 1) * SCAN_LANES)
            ar = jnp.broadcast_to(apr_ref[0:1, st], (SUBLANES, SCAN_LANES))
            ai = jnp.broadcast_to(api_ref[0:1, st], (SUBLANES, SCAN_LANES))

            def scan_body(j, carry, re=re, im=im, ar=ar, ai=ai):
                rows = pl.ds(pl.multiple_of(j * SUBLANES, SUBLANES), SUBLANES)
                hr, hi = _cmul_add(ar, ai, carry[0], carry[1], x_ref[rows, re], x_ref[rows, im])
                x_ref[rows, re] = hr
                x_ref[rows, im] = hi
                return hr, hi

            zero = jnp.zeros((SUBLANES, SCAN_LANES), F32)
            er, ei = lax.fori_loop(0, nj, scan_body, (zero, zero), unroll=2)

            alr, ali = apr_ref[nj - 1:nj, st], api_ref[nj - 1:nj, st]
            cr, ci = hcr_ref[:, st], hci_ref[:, st]
            crs, cis = [], []
            for s in range(SUBLANES):
                crs.append(cr)
                cis.append(ci)
                cr, ci = _cmul_add(alr, ali, cr, ci, er[s:s + 1, :], ei[s:s + 1, :])
            hcr_ref[:, st] = cr
            hci_ref[:, st] = ci
            cin_r = jnp.concatenate(crs, axis=0)
            cin_i = jnp.concatenate(cis, axis=0)

            def fix_body(j, _, re=re, im=im, st=st, cin_r=cin_r, cin_i=cin_i):
                rows = pl.ds(pl.multiple_of(j * SUBLANES, SUBLANES), SUBLANES)
                pr = jnp.broadcast_to(apr_ref[pl.ds(j, 1), st], (SUBLANES, SCAN_LANES))
                pi = jnp.broadcast_to(api_ref[pl.ds(j, 1), st], (SUBLANES, SCAN_LANES))
                hr, hi = _cmul_add(pr, pi, cin_r, cin_i, x_ref[rows, re], x_ref[rows, im])
                x_ref[rows, re] = hr
                x_ref[rows, im] = hi
                return 0

            lax.fori_loop(0, nj, fix_body, 0, unroll=2)
        _s5_readout(q, x_ref, u_ref, cblk_ref, d_ref, z_ref)

    o_ref[...] = _s5_glu(z_ref, wglu_ref, bglu_ref).astype(o_ref.dtype)
    hr_out_ref[...] = hcr_ref[...]
    hi_out_ref[...] = hci_ref[...]


def _s5_sample_kernel(u_ref, h0r_ref, h0i_ref, bblk_ref, cblk_ref, d_ref, apr_ref, api_ref, wglu_ref, bglu_ref,
                      o_ref, hr_out_ref, hi_out_ref, x_ref, z_ref, *, nb, nt):
    n_blocks = bblk_ref.shape[0]
    for q in range(n_blocks):
        ub = u_ref[:, q * SSM_BLOCK_CH:(q + 1) * SSM_BLOCK_CH]
        x_ref[...] = jnp.dot(ub.astype(BF16), bblk_ref[q], preferred_element_type=F32)
        for c in range(SSM_BLOCK_ST // SCAN_LANES):
            re = slice(c * SCAN_LANES, (c + 1) * SCAN_LANES)
            im = slice(SSM_BLOCK_ST + c * SCAN_LANES, SSM_BLOCK_ST + (c + 1) * SCAN_LANES)
            st = slice(q * SSM_BLOCK_ST + c * SCAN_LANES, q * SSM_BLOCK_ST + (c + 1) * SCAN_LANES)
            ar, ai = apr_ref[0:1, st], api_ref[0:1, st]
            hr, hi = h0r_ref[:, st], h0i_ref[:, st]
            for t in range(nt):
                rows = slice(t * nb, (t + 1) * nb)
                hr, hi = _cmul_add(ar, ai, hr, hi, x_ref[rows, re], x_ref[rows, im])
                x_ref[rows, re] = hr
                x_ref[rows, im] = hi
            hr_out_ref[:, st] = hr
            hi_out_ref[:, st] = hi
        _s5_readout(q, x_ref, u_ref, cblk_ref, d_ref, z_ref)
    o_ref[...] = _s5_glu(z_ref, wglu_ref, bglu_ref).astype(o_ref.dtype)


def _s5_params(lam_re, lam_im, log_step, b_re, b_im, c_re, c_im, d_skip, n_pow):
    g, p = lam_re.shape
    dt = jnp.exp(log_step)[:, None]
    steps = jnp.arange(1, n_pow + 1, dtype=F32)[:, None, None]
    mag = jnp.exp(lam_re * dt * steps)
    phase = lam_im * dt * steps
    ap_re = (mag * jnp.cos(phase)).reshape(n_pow, g * p)
    ap_im = (mag * jnp.sin(phase)).reshape(n_pow, g * p)
    a_re, a_im = mag[0] * jnp.cos(phase[0]), mag[0] * jnp.sin(phase[0])
    num_re = a_re - 1.0
    den = lam_re * lam_re + lam_im * lam_im
    k_re = (num_re * lam_re + a_im * lam_im) / den
    k_im = (a_im * lam_re - num_re * lam_im) / den
    bp_re = k_re[..., None] * b_re - k_im[..., None] * b_im
    bp_im = k_re[..., None] * b_im + k_im[..., None] * b_re
    nblk = g // SSM_BLOCK_GROUPS
    eye = jnp.eye(SSM_BLOCK_GROUPS, dtype=F32)

    def b_block(bp):
        bp = bp.reshape(nblk, SSM_BLOCK_GROUPS, p, SSM_GROUP)
        return jnp.einsum('qgpc,gh->qgchp', bp, eye).reshape(nblk, SSM_BLOCK_CH, SSM_BLOCK_ST)

    def c_block(c):
        c = c.reshape(nblk, SSM_BLOCK_GROUPS, SSM_GROUP, p)
        return jnp.einsum('qgcp,gh->qgphc', c, eye).reshape(nblk, SSM_BLOCK_ST, SSM_BLOCK_CH)

    bblk = jnp.concatenate([b_block(bp_re), b_block(bp_im)], axis=2).astype(BF16)
    cblk = jnp.concatenate([c_block(c_re), -c_block(c_im)], axis=1).astype(BF16)
    return bblk, cblk, d_skip.reshape(1, g * SSM_GROUP), ap_re, ap_im


def _const_spec(shape):
    return pl.BlockSpec(shape, lambda *_: (0,) * len(shape))


def s5_prompt(u, h0r, h0i, params, w_glu, b_glu, chunk):
    t, w = u.shape
    nj = chunk // SUBLANES
    bblk, cblk, d, apr, api = params
    ns = apr.shape[1]
    consts = [h0r, h0i, bblk, cblk, d, apr, api, w_glu, b_glu.reshape(1, w)]
    u_perm = u.reshape(t // chunk, SUBLANES, nj, w).transpose(0, 2, 1, 3).reshape(t, w)
    o_perm, h_re, h_im = pl.pallas_call(
        functools.partial(_s5_prompt_kernel, nj=nj),
        grid=(t // chunk,),
        in_specs=[pl.BlockSpec((chunk, w), lambda i: (i, 0))] + [_const_spec(c.shape) for c in consts],
        out_specs=[pl.BlockSpec((chunk, w), lambda i: (i, 0)), _const_spec((1, ns)), _const_spec((1, ns))],
        out_shape=[jax.ShapeDtypeStruct((t, w), BF16), jax.ShapeDtypeStruct((1, ns), F32),
                   jax.ShapeDtypeStruct((1, ns), F32)],
        scratch_shapes=[pltpu.VMEM((chunk, 2 * SSM_BLOCK_ST), F32), pltpu.VMEM((chunk, w), F32),
                        pltpu.VMEM((1, ns), F32), pltpu.VMEM((1, ns), F32)],
        compiler_params=_cparams("arbitrary"),
        name="s5_prompt",
    )(u_perm, *consts)
    o = o_perm.reshape(t // chunk, nj, SUBLANES, w).transpose(0, 2, 1, 3).reshape(t, w)
    return o, h_re, h_im


def s5_sample(u_tb, h0r, h0i, params, w_glu, b_glu, nb, nt):
    m, w = u_tb.shape
    bblk, cblk, d, apr, api = params
    ns = apr.shape[1]
    consts = [h0r, h0i, bblk, cblk, d, apr, api, w_glu, b_glu.reshape(1, w)]
    return pl.pallas_call(
        functools.partial(_s5_sample_kernel, nb=nb, nt=nt),
        grid=(1,),
        in_specs=[_const_spec(u_tb.shape)] + [_const_spec(c.shape) for c in consts],
        out_specs=[_const_spec((m, w)), _const_spec((nb, ns)), _const_spec((nb, ns))],
        out_shape=[jax.ShapeDtypeStruct((m, w), BF16), jax.ShapeDtypeStruct((nb, ns), F32),
                   jax.ShapeDtypeStruct((nb, ns), F32)],
        scratch_shapes=[pltpu.VMEM((m, 2 * SSM_BLOCK_ST), F32), pltpu.VMEM((m, w), F32)],
        compiler_params=_cparams("arbitrary"),
        name="s5_sample",
    )(u_tb, *consts)


def _log_sigmoid_pair(w):
    lk = jnp.minimum(w, 0.0) - jnp.log(1.0 + jnp.exp(-jnp.abs(w)))
    return lk - w, lk


def _split3_bf16(x):
    hi = x.astype(BF16).astype(F32)
    mid = (x - hi).astype(BF16).astype(F32)
    lo = (x - hi - mid).astype(BF16).astype(F32)
    return hi, mid, lo


def _sb_prompt_kernel(bias_ref, q_ref, k_ref, v_ref, o_ref, acc_ref, carry_ref, w_ref, *, tq, tk, n_hp):
    hp, i = pl.program_id(0), pl.program_id(1)
    kb_per_q = tq // tk
    row = lax.broadcasted_iota(jnp.int32, (tq, tk), 0)
    col = lax.broadcasted_iota(jnp.int32, (tq, tk), 1)
    newer = (lax.broadcasted_iota(jnp.int32, (tk, tk), 0)
             > lax.broadcasted_iota(jnp.int32, (tk, tk), 1)).astype(BF16)
    acc_ref[...] = jnp.zeros_like(acc_ref)
    carry_ref[...] = jnp.zeros_like(carry_ref)

    lane_q = lax.broadcasted_iota(jnp.int32, (tq, SB_HEAD_DIM), 1)
    lane_k = lax.broadcasted_iota(jnp.int32, (tk, SB_HEAD_DIM), 1)
    q_units = jnp.where(lane_q < 3, 1.0, 0.0).astype(BF16)
    q_aug, k_bias = [], []
    for hh in range(n_hp):
        hi, mid, lo = _split3_bf16(jnp.full((tk, SB_HEAD_DIM), -bias_ref[hp * n_hp + hh], F32))
        k_bias.append(jnp.where(lane_k == 0, hi, jnp.where(lane_k == 1, mid, jnp.where(lane_k == 2, lo, 0.0)))
                      .astype(BF16))
        q_aug.append(jnp.concatenate([q_ref[:, hh * SB_HEAD_DIM:(hh + 1) * SB_HEAD_DIM], q_units], axis=1))

    def logits(kb, hh):
        rows = pl.ds(pl.multiple_of(kb * tk, tk), tk)
        k_aug = jnp.concatenate([k_ref[rows, hh * SB_HEAD_DIM:(hh + 1) * SB_HEAD_DIM], k_bias[hh]], axis=1)
        return lax.dot_general(q_aug[hh], k_aug, (((1,), (1,)), ((), ())), preferred_element_type=F32)

    def weigh(kb, hh, w, mask):
        rows = pl.ds(pl.multiple_of(kb * tk, tk), tk)
        ls_s, lk_s, lkb_s = [], [], []
        for s in range(tq // STRIP):
            r = slice(s * STRIP, (s + 1) * STRIP)
            ls, lk = _log_sigmoid_pair(w[r])
            if mask is not None:
                lk = jnp.where(mask[r], lk, 0.0)
            ls_s.append(ls)
            lk_s.append(jnp.sum(lk, axis=1, keepdims=True))
            lkb_s.append(lk.astype(BF16))
        cum = jnp.dot(jnp.concatenate(lkb_s, axis=0), newer, preferred_element_type=F32)
        a_s = []
        for s in range(tq // STRIP):
            r = slice(s * STRIP, (s + 1) * STRIP)
            a = jnp.exp(ls_s[s] + cum[r] + carry_ref[hh, r, :])
            if mask is not None:
                a = jnp.where(mask[r], a, 0.0)
            a_s.append(a.astype(BF16))
        acc_ref[hh] += jnp.dot(jnp.concatenate(a_s, axis=0), v_ref[rows, hh * SB_HEAD_DIM:(hh + 1) * SB_HEAD_DIM],
                               preferred_element_type=F32)
        carry_ref[hh] += jnp.concatenate(lk_s, axis=0)

    def tile(kb, mask):
        for hh in range(n_hp):
            weigh(kb, hh, logits(kb, hh), mask)

    for d in reversed(range(kb_per_q)):
        tile(i * kb_per_q + d, col + d * tk < row)

    for hh in range(n_hp):
        w_ref[hh] = logits(jnp.maximum(i * kb_per_q - 1, 0), hh)

    def body(jj, _):
        base = (i - jj) * kb_per_q
        for hh in range(n_hp):
            weigh(base - 1, hh, w_ref[hh], None)
        for d in range(1, kb_per_q):
            tile(base - 1 - d, None)
        for hh in range(n_hp):
            w_ref[hh] = logits(jnp.maximum(base - kb_per_q - 1, 0), hh)
        return 0

    lax.fori_loop(0, i, body, 0)
    for hh in range(n_hp):
        o_ref[:, hh * SB_HEAD_DIM:(hh + 1) * SB_HEAD_DIM] = acc_ref[hh].astype(o_ref.dtype)


def sb_attention_prompt(q, k, v, bias, tq=512, tk=256, heads_per_step=2):
    t, w = q.shape
    n_heads = w // SB_HEAD_DIM
    tq, tk, n_hp = _tile(t, tq), _tile(t, tk), heads_per_step
    assert tq % tk == 0 and n_heads % n_hp == 0
    wb = n_hp * SB_HEAD_DIM
    grid_spec = pltpu.PrefetchScalarGridSpec(
        num_scalar_prefetch=1,
        grid=(n_heads // n_hp, t // tq),
        in_specs=[pl.BlockSpec((tq, wb), lambda h, i, b: (i, h)),
                  pl.BlockSpec((t, wb), lambda h, i, b: (0, h)),
                  pl.BlockSpec((t, wb), lambda h, i, b: (0, h))],
        out_specs=pl.BlockSpec((tq, wb), lambda h, i, b: (i, h)),
        scratch_shapes=[pltpu.VMEM((n_hp, tq, SB_HEAD_DIM), F32), pltpu.VMEM((n_hp, tq, 1), F32),
                        pltpu.VMEM((n_hp, tq, tk), F32)],
    )
    return pl.pallas_call(
        functools.partial(_sb_prompt_kernel, tq=tq, tk=tk, n_hp=n_hp),
        grid_spec=grid_spec,
        out_shape=jax.ShapeDtypeStruct((t, w), BF16),
        compiler_params=_cparams("parallel", "arbitrary"),
        name="sb_attention_prompt",
    )(bias, q, k, v)


def _sb_sample_kernel(pt_ref, *refs, n_pp, n_heads):
    wq_ref, bias_ref, kn_ref, vn_ref = refs[:4]
    kp_refs = refs[4:4 + n_pp]
    vp_refs = refs[4 + n_pp:4 + 2 * n_pp]
    o_ref, acc_ref, carry_ref = refs[4 + 2 * n_pp:]
    p = pl.program_id(1)
    tk = PAGE_SIZE
    nq = o_ref.shape[0]
    dh = SB_HEAD_DIM
    newer = (lax.broadcasted_iota(jnp.int32, (tk, tk), 1)
             > lax.broadcasted_iota(jnp.int32, (tk, tk), 0)).astype(BF16)

    def head_rows(ref, h):
        return ref[pl.ds(h, tk, stride=n_heads), :]

    def weights(z, mask, carry):
        ls, lk = _log_sigmoid_pair(z)
        if mask is not None:
            lk = jnp.where(mask, lk, 0.0)
        after = jnp.dot(newer, lk.astype(BF16), preferred_element_type=F32) + carry
        a = jnp.exp(ls + after)
        if mask is not None:
            a = jnp.where(mask, a, 0.0)
        return a.T, jnp.sum(lk, axis=0, keepdims=True)

    @pl.when(p == 0)
    def _():
        lane_q = lax.broadcasted_iota(jnp.int32, (tk, LANES), 1) % nq
        key = lax.broadcasted_iota(jnp.int32, (tk, LANES), 0)
        fill = jnp.zeros((tk - nq, kn_ref.shape[1]), F32)
        kn = jnp.concatenate([kn_ref[...], fill], axis=0).astype(BF16)
        vn = jnp.concatenate([vn_ref[...], fill], axis=0).astype(BF16)
        z = jnp.dot(kn, wq_ref[...], preferred_element_type=F32) - bias_ref[...]
        at, tot = weights(z, key < lane_q, jnp.zeros((1, LANES), F32))
        carry_ref[...] = tot
        for h in range(n_heads):
            acc_ref[h * nq:(h + 1) * nq, :] = jnp.dot(at[h * nq:(h + 1) * nq, :].astype(BF16),
                                                      vn[:, h * dh:(h + 1) * dh], preferred_element_type=F32)

    per = n_pp // SAMPLE_SPLITS
    carry = carry_ref[...]
    partial = [None] * n_heads
    for sp in range(SAMPLE_SPLITS):
        pages = list(range(sp * per, (sp + 1) * per))
        z = -bias_ref[...]
        for hp in range(n_heads // 2):
            lhs = jnp.concatenate(
                [jnp.concatenate([head_rows(kp_refs[r], 2 * hp), head_rows(kp_refs[r], 2 * hp + 1)], axis=1)
                 for r in pages], axis=0).astype(BF16)
            z = z + jnp.dot(lhs, wq_ref[2 * hp * dh:(2 * hp + 2) * dh, :], preferred_element_type=F32)
        ats = []
        for idx in range(per):
            at, tot = weights(z[idx * tk:(idx + 1) * tk, :], None, carry)
            carry = carry + tot
            ats.append(at)
        for h in range(n_heads):
            lhs = jnp.concatenate([at[h * nq:(h + 1) * nq, :] for at in ats], axis=1).astype(BF16)
            rhs = jnp.concatenate([head_rows(vp_refs[r], h) for r in pages], axis=0).astype(BF16)
            term = jnp.dot(lhs, rhs, preferred_element_type=F32)
            partial[h] = term if partial[h] is None else partial[h] + term
    carry_ref[...] = carry
    for h in range(n_heads):
        acc_ref[h * nq:(h + 1) * nq, :] += partial[h]

    @pl.when(p == pl.num_programs(1) - 1)
    def _():
        for h in range(n_heads):
            o_ref[:, h * dh:(h + 1) * dh] = acc_ref[h * nq:(h + 1) * nq, :]


def sb_attention_sample(q, k_new, v_new, bias, cache_k, cache_v, page_table, pages_per_step=8):
    nb, nq, w = q.shape
    n_heads = w // SB_HEAD_DIM
    n_pages = page_table.shape[1]
    n_pp = pages_per_step
    assert n_pages % n_pp == 0 and n_heads * nq <= LANES and nq <= PAGE_SIZE and n_heads % 2 == 0
    assert LANES % nq == 0 and nq % SUBLANES == 0
    q4 = q.reshape(nb, nq, n_heads, SB_HEAD_DIM).transpose(0, 2, 3, 1)
    same_head = jnp.arange(n_heads)[:, None] == jnp.arange(LANES // nq)[None, :]
    wq = jnp.where(same_head[None, :, None, :, None], q4[:, :, :, None, :], 0).reshape(nb, w, LANES)
    bias_row = jnp.pad(jnp.repeat(bias, nq), (0, LANES - n_heads * nq)).reshape(1, LANES)

    page_rows = PAGE_SIZE * n_heads

    def page_spec(r):
        return pl.BlockSpec((None, page_rows, SB_HEAD_DIM),
                            lambda b, p, pt: (pt[b, n_pages - 1 - (p * n_pp + r)], 0, 0))

    per_b = lambda b, p, pt: (b, 0, 0)
    grid_spec = pltpu.PrefetchScalarGridSpec(
        num_scalar_prefetch=1,
        grid=(nb, n_pages // n_pp),
        in_specs=[pl.BlockSpec((None, w, LANES), per_b), pl.BlockSpec((1, LANES), lambda b, p, pt: (0, 0)),
                  pl.BlockSpec((None, nq, w), per_b), pl.BlockSpec((None, nq, w), per_b)]
                 + [page_spec(r) for r in range(n_pp)] + [page_spec(r) for r in range(n_pp)],
        out_specs=pl.BlockSpec((None, nq, w), per_b),
        scratch_shapes=[pltpu.VMEM((n_heads * nq, SB_HEAD_DIM), F32), pltpu.VMEM((1, LANES), F32)],
    )
    return pl.pallas_call(
        functools.partial(_sb_sample_kernel, n_pp=n_pp, n_heads=n_heads),
        grid_spec=grid_spec,
        out_shape=jax.ShapeDtypeStruct((nb, nq, w), F32),
        compiler_params=_cparams("parallel", "arbitrary"),
        name="sb_attention_sample",
    )(page_table, wq, bias_row, k_new, v_new, *([cache_k] * n_pp), *([cache_v] * n_pp))


def _mem_attn_kernel(q_ref, k_ref, v_ref, o_ref, *, scale):
    dh = o_ref.shape[1] // MEM_HEADS
    for h in range(MEM_HEADS):
        cols = slice(h * dh, (h + 1) * dh)
        kh = k_ref[:, cols].astype(BF16)
        vh = v_ref[:, cols].astype(BF16)
        s = lax.dot_general(q_ref[:, cols], kh, (((1,), (1,)), ((), ())), preferred_element_type=F32) * scale
        e = jnp.exp(s - jnp.max(s, axis=-1, keepdims=True))
        p = e / jnp.sum(e, axis=-1, keepdims=True)
        o_ref[:, cols] = jnp.dot(p.astype(BF16), vh, preferred_element_type=F32).astype(o_ref.dtype)


def memory_attention(q, mem_k, mem_v, tq=512):
    nb, t, w = q.shape
    n_mem = mem_k.shape[1]
    tq = _tile(t, tq)
    return pl.pallas_call(
        functools.partial(_mem_attn_kernel, scale=(w // MEM_HEADS) ** -0.5),
        grid=(nb, t // tq),
        in_specs=[pl.BlockSpec((None, tq, w), lambda b, i: (b, i, 0)),
                  pl.BlockSpec((None, n_mem, w), lambda b, i: (b, 0, 0)),
                  pl.BlockSpec((None, n_mem, w), lambda b, i: (b, 0, 0))],
        out_specs=pl.BlockSpec((None, tq, w), lambda b, i: (b, i, 0)),
        out_shape=jax.ShapeDtypeStruct((nb, t, w), BF16),
        compiler_params=_cparams("parallel", "arbitrary"),
        name="memory_attention",
    )(q, mem_k, mem_v)


def _merge_kernel(hn_ref, o0_ref, o1_ref, o2_ref, g0_ref, g1_ref, g2_ref, b0_ref, b1_ref, b2_ref, out_ref):
    hn = hn_ref[...]
    total = None
    for o_ref, g_ref, b_ref in ((o0_ref, g0_ref, b0_ref), (o1_ref, g1_ref, b1_ref), (o2_ref, g2_ref, b2_ref)):
        gate = _sigmoid(jnp.dot(hn, g_ref[...], preferred_element_type=F32))
        term = gate * jnp.dot(o_ref[...], b_ref[...], preferred_element_type=F32)
        total = term if total is None else total + term
    out_ref[...] = total.astype(out_ref.dtype)


def gated_merge(hn, branches, w_gates, w_branches, tm=512, tn=512):
    m, d = hn.shape
    tm, tn = _tile(m, tm), _tile(d, tn)
    nj = d // tn
    row = lambda i, j: (i, 0)
    in_specs = [pl.BlockSpec((tm, d), row)]
    in_specs += [pl.BlockSpec((tm, o.shape[1]), row) for o in branches]
    in_specs += [pl.BlockSpec((d, tn), functools.partial(lambda i, j, b: (0, b * nj + j), b=b)) for b in range(3)]
    in_specs += [pl.BlockSpec((w.shape[0], tn), lambda i, j: (0, j)) for w in w_branches]
    return pl.pallas_call(
        _merge_kernel,
        grid=(m // tm, nj),
        in_specs=in_specs,
        out_specs=pl.BlockSpec((tm, tn), lambda i, j: (i, j)),
        out_shape=jax.ShapeDtypeStruct((m, d), BF16),
        compiler_params=_cparams("parallel", "arbitrary"),
        name="gated_merge",
    )(hn, *branches, w_gates, w_gates, w_gates, *w_branches)


MOE_TILE = 1024
MOE_ROWS = 320
MOE_EXPERTS_PER_STEP = 2
ROUTER_GROUP_LANE = 64
ROUTER_RANK_LANE = 65


def _router_kernel(x_ref, g_ref, wr_ref, br_ref, hn_ref, gate_ref, cnt_ref, run_ref):
    hn = _rms(x_ref[...], g_ref[...])
    hn_ref[...] = hn.astype(hn_ref.dtype)
    logits = jnp.dot(hn, wr_ref[...], preferred_element_type=F32, precision=lax.Precision.HIGHEST) + br_ref[...]
    lane = lax.broadcasted_iota(jnp.int32, logits.shape, 1).astype(F32)
    neg, far = -1e30, 1e9
    rmax = lambda v: jnp.max(v, axis=-1, keepdims=True)
    rmin = lambda v: jnp.min(v, axis=-1, keepdims=True)
    rsum = lambda v: jnp.sum(v, axis=-1, keepdims=True)
    is_group = lane < N_GROUPS
    gl = jnp.where(is_group, logits, neg)
    gmax = rmax(gl)
    g_val = 1.0 / rsum(jnp.where(is_group, jnp.exp(gl - gmax), 0.0))
    g_idx = rmin(jnp.where(gl == gmax, lane, far))
    lo = ROUTER_EXPERT_LANE0 + EXPERTS_PER_GROUP * g_idx
    in_group = jnp.logical_and(lane >= lo, lane < lo + EXPERTS_PER_GROUP)
    el = jnp.where(in_group, logits, neg)
    ex = jnp.where(in_group, jnp.exp(el - rmax(el)), 0.0)
    prob = jnp.where(in_group, ex / rsum(ex), -1.0)
    v1 = rmax(prob)
    i1 = rmin(jnp.where(prob == v1, lane, far))
    rest = jnp.where(lane == i1, -1.0, prob)
    v2 = rmax(rest)
    i2 = rmin(jnp.where(rest == v2, lane, far))
    within = jnp.where(lane == i1, v1 / (v1 + v2), jnp.where(lane == i2, v2 / (v1 + v2), 0.0))

    @pl.when(pl.program_id(1) == 0)
    def _():
        run_ref[...] = jnp.zeros_like(run_ref)

    tm = logits.shape[0]
    chose = jnp.where(lane == g_idx, 1.0, 0.0)
    earlier = (lax.broadcasted_iota(jnp.int32, (tm, tm), 1)
               < lax.broadcasted_iota(jnp.int32, (tm, tm), 0)).astype(BF16)
    before = jnp.dot(earlier, chose.astype(BF16), preferred_element_type=F32) + run_ref[...]
    rank = rsum(chose * before)
    run_ref[...] += jnp.sum(chose, axis=0, keepdims=True)
    cnt_ref[...] = run_ref[...]
    gate_ref[...] = (g_val * within + jnp.where(lane == ROUTER_GROUP_LANE, g_idx, 0.0)
                     + jnp.where(lane == ROUTER_RANK_LANE, rank, 0.0))


def moe_router(x, g, w_router, b_router, tile, tm=256):
    m, d = x.shape
    tm = _tile(tile, tm)
    sub = tile // tm
    row = lambda i, s: (i * sub + s, 0)
    return pl.pallas_call(
        _router_kernel,
        grid=(m // tile, sub),
        in_specs=[pl.BlockSpec((tm, d), row), _const_spec((1, d)), _const_spec((d, LANES)), _const_spec((1, LANES))],
        out_specs=[pl.BlockSpec((tm, d), row), pl.BlockSpec((tm, LANES), row),
                   pl.BlockSpec((None, 1, LANES), lambda i, s: (i, 0, 0))],
        out_shape=[jax.ShapeDtypeStruct((m, d), BF16), jax.ShapeDtypeStruct((m, LANES), F32),
                   jax.ShapeDtypeStruct((m // tile, 1, LANES), F32)],
        scratch_shapes=[pltpu.VMEM((1, LANES), F32)],
        compiler_params=_cparams("parallel", "arbitrary"),
        name="moe_router",
    )(x, g.reshape(1, d), w_router, b_router)


def _experts_kernel(bg_ref, nb_ref, hn_ref, gate_ref, x_ref, drow_ref, dcol_ref, wgu_ref, wd_ref, gf_ref, o_ref,
                    xs_ref, gs_ref, yb_ref):
    i, b, j = pl.program_id(0), pl.program_id(1), pl.program_id(2)
    tile = hn_ref.shape[0]
    rows = xs_ref.shape[0]
    ff = wd_ref.shape[1]
    active = b < nb_ref[i]
    last_j = j == pl.num_programs(2) - 1

    @pl.when(jnp.logical_and(b == 0, j == 0))
    def _():
        o_ref[...] = x_ref[...]

    @pl.when(jnp.logical_and(active, j == 0))
    def _():
        slot = lax.broadcasted_iota(jnp.int32, (rows, tile), 0) + b * rows
        pick = jnp.where(drow_ref[...] == slot, 1.0, 0.0).astype(BF16)
        xs_ref[...] = jnp.dot(pick, hn_ref[...], preferred_element_type=F32).astype(BF16)
        gs = None
        for part in _split3_bf16(gate_ref[...]):
            term = jnp.dot(pick, part.astype(BF16), preferred_element_type=F32)
            gs = term if gs is None else gs + term
        gs_ref[...] = gs
        yb_ref[...] = jnp.zeros_like(yb_ref)

    @pl.when(active)
    def _():
        lane = lax.broadcasted_iota(jnp.int32, gs_ref.shape, 1)
        y = yb_ref[...]
        for k in range(MOE_EXPERTS_PER_STEP):
            e_lane = ROUTER_EXPERT_LANE0 + bg_ref[i, b] * EXPERTS_PER_GROUP + j * MOE_EXPERTS_PER_STEP + k
            gate = jnp.sum(jnp.where(lane == e_lane, gs_ref[...], 0.0), axis=-1, keepdims=True)
            au = jnp.dot(xs_ref[...], wgu_ref[k], preferred_element_type=F32)
            a, up = au[:, :ff], au[:, ff:]
            act = a * _sigmoid(a) * up * gate
            y = y + jnp.dot(act.astype(BF16), wd_ref[k], preferred_element_type=F32)
        yb_ref[...] = y

    @pl.when(jnp.logical_and(active, last_j))
    def _():
        slot = lax.broadcasted_iota(jnp.int32, (tile, rows), 1) + b * rows
        put = jnp.where(dcol_ref[...] == slot, 1.0, 0.0).astype(BF16)
        o_ref[...] += jnp.dot(put, yb_ref[...].astype(BF16), preferred_element_type=F32)

    @pl.when(jnp.logical_and(b == pl.num_programs(1) - 1, last_j))
    def _():
        o_ref[...] = _rms(o_ref[...], gf_ref[...])


def moe_experts_final_norm(hn, route, counts, x, w_gate_up, w_down, g_final, tile):
    m, d = x.shape
    n_exp, _, ff2 = w_gate_up.shape
    n_tiles = m // tile
    rows = min(MOE_ROWS, tile)
    n_blocks = (tile + N_GROUPS * (rows - 1)) // rows
    per = MOE_EXPERTS_PER_STEP
    steps = EXPERTS_PER_GROUP // per
    w_gate_up = w_gate_up.reshape(n_exp // per, per, d, ff2)
    w_down = w_down.reshape(n_exp // per, per, ff2 // 2, d)
    cnt = counts[:, 0, :N_GROUPS].astype(jnp.int32)
    blocks_g = (cnt + rows - 1) // rows
    end_g = jnp.cumsum(blocks_g, axis=1)
    start_g = end_g - blocks_g
    n_blk = end_g[:, -1]
    bidx = jnp.arange(n_blocks, dtype=jnp.int32)[None, :]
    blk_group = jnp.sum(bidx[:, :, None] >= end_g[:, None, :], axis=2).astype(jnp.int32)
    last_group = jnp.sum(jnp.where(bidx == jnp.maximum(n_blk - 1, 0)[:, None], blk_group, 0), axis=1, keepdims=True)
    blk_group = jnp.minimum(blk_group, last_group)
    grp = route[:, ROUTER_GROUP_LANE].astype(jnp.int32).reshape(n_tiles, tile)
    rank = route[:, ROUTER_RANK_LANE].astype(jnp.int32).reshape(n_tiles, tile)
    start_t = sum(jnp.where(grp == g, start_g[:, g:g + 1], 0) for g in range(N_GROUPS))
    dest = rows * start_t + rank

    def w_index(i, b, j, bg, nb):
        live = b < nb[i]
        last = jnp.maximum(nb[i] - 1, 0)
        return (jnp.where(live, bg[i, b] * steps + j, bg[i, last] * steps + steps - 1), 0, 0, 0)

    row = lambda i, b, j, bg, nb: (i, 0)
    grid_spec = pltpu.PrefetchScalarGridSpec(
        num_scalar_prefetch=2,
        grid=(n_tiles, n_blocks, steps),
        in_specs=[pl.BlockSpec((tile, d), row, pipeline_mode=pl.Buffered(1)), pl.BlockSpec((tile, LANES), row),
                  pl.BlockSpec((tile, d), row, pipeline_mode=pl.Buffered(1)),
                  pl.BlockSpec((None, 1, tile), lambda i, b, j, bg, nb: (i, 0, 0)),
                  pl.BlockSpec((tile, 1), row),
                  pl.BlockSpec((None, per, d, ff2), w_index), pl.BlockSpec((None, per, ff2 // 2, d), w_index),
                  pl.BlockSpec((1, d), lambda i, b, j, bg, nb: (0, 0))],
        out_specs=pl.BlockSpec((tile, d), row),
        scratch_shapes=[pltpu.VMEM((rows, d), BF16), pltpu.VMEM((rows, LANES), F32), pltpu.VMEM((rows, d), F32)],
    )
    return pl.pallas_call(
        _experts_kernel,
        grid_spec=grid_spec,
        out_shape=jax.ShapeDtypeStruct((m, d), F32),
        compiler_params=_cparams("parallel", "arbitrary", "arbitrary"),
        name="moe_experts",
    )(blk_group, n_blk, hn, route, x, dest.reshape(n_tiles, 1, tile), dest.reshape(m, 1), w_gate_up, w_down,
      g_final.reshape(1, d))


S5_CHUNK = 512


def kernel(x_prompt, x_sample, mem_prompt, cache_k, cache_v, page_table, cache_mem_k, cache_mem_v, state_ssm_re, state_ssm_im, g_mix_norm, w_in, sb_logit_bias, ssm_lambda_re, ssm_lambda_im, ssm_log_step, ssm_b_re, ssm_b_im, ssm_c_re, ssm_c_im, ssm_d, w_glu, b_glu, g_mem_norm, w_mem_k, w_mem_v, w_branch_ssm, w_branch_sb, w_branch_mem, w_out, g_ffn_norm, w_group_router, b_group_router, w_expert_router, b_expert_router, w_exp_gate, w_exp_up, w_exp_down, g_final):
    depth = w_in.shape[0]
    assert depth == 1, "one mixer + MoE layer per step"
    bp, tp, d = x_prompt.shape
    bs, ts, _ = x_sample.shape
    assert bp == 1
    n_groups, n_state = ssm_lambda_re.shape[1:]
    ssm_w = n_groups * SSM_GROUP
    n_st = n_groups * n_state
    sb_w = cache_k.shape[3] * cache_k.shape[4]
    mem_w = cache_mem_k.shape[3] * cache_mem_k.shape[4]
    n_mem = mem_prompt.shape[1]
    assert ssm_w == sb_w == mem_w
    n_heads = sb_w // SB_HEAD_DIM
    mem_dh = mem_w // MEM_HEADS
    l = 0

    w5 = w_in[l, :, :5 * ssm_w].astype(BF16)
    w_gates = w_in[l, :, 5 * ssm_w:].astype(BF16)
    w_branches = [w_branch_ssm[l].astype(BF16), w_branch_sb[l].astype(BF16), w_branch_mem[l].astype(BF16)]
    w_out_b = w_out[l].astype(BF16)
    w_glu_b = w_glu[l].astype(BF16)
    w_memkv = jnp.concatenate([w_mem_k[l], w_mem_v[l]], axis=1).astype(BF16)
    w_gate_up = jnp.concatenate([w_exp_gate[l], w_exp_up[l]], axis=2).astype(BF16)
    w_down = w_exp_down[l].astype(BF16)
    def router_lanes(group_part, expert_part):
        rows = group_part.shape[0]
        return jnp.concatenate(
            [group_part, jnp.zeros((rows, ROUTER_EXPERT_LANE0 - N_GROUPS), F32), expert_part,
             jnp.zeros((rows, LANES - ROUTER_EXPERT_LANE0 - N_EXPERTS), F32)], axis=1)

    w_router = router_lanes(w_group_router[l], w_expert_router[l])
    b_router = router_lanes(b_group_router[l][None, :], b_expert_router[l][None, :])
    s5p = _s5_params(ssm_lambda_re[l], ssm_lambda_im[l], ssm_log_step[l], ssm_b_re[l], ssm_b_im[l],
                     ssm_c_re[l], ssm_c_im[l], ssm_d[l], S5_CHUNK // SUBLANES)
    bias = sb_logit_bias[l]

    mn = rmsnorm(mem_prompt.reshape(n_mem, d), g_mem_norm[l], BF16)
    (mem_kv,) = matmul(mn, w_memkv, (F32,), tm=n_mem, tn=1024, name="memory_kv")
    mk_p, mv_p = mem_kv[:, :mem_w], mem_kv[:, mem_w:]

    def mixer_and_moe(x, attend, mem_k, mem_v, s5):
        hn = rmsnorm(x, g_mix_norm[l], BF16)
        u, q, k, kb, v, vb, qm = in_projection(hn, w5, ssm_w)
        o_ssm, h_re, h_im = s5(u)
        o_sb = attend(q, k, v, kb, vb)
        o_mem = memory_attention(qm.reshape(mem_k.shape[0], -1, mem_w), mem_k, mem_v).reshape(-1, mem_w)
        merged = gated_merge(hn, (o_ssm, o_sb, o_mem), w_gates, w_branches)
        (x2,) = matmul(merged, w_out_b, (F32,), res=x, tm=512, tn=1024, name="out_projection")
        moe_tile = min(MOE_TILE, x2.shape[0])
        hn2, route, counts = moe_router(x2, g_ffn_norm[l], w_router, b_router, moe_tile)
        y = moe_experts_final_norm(hn2, route, counts, x2, w_gate_up, w_down, g_final, moe_tile)
        return y, k, v, h_re, h_im

    zeros = jnp.zeros((1, n_st), F32)
    y_p, k_p, v_p, hr_p, hi_p = mixer_and_moe(
        x_prompt.reshape(tp, d),
        lambda q, k, v, kb, vb: sb_attention_prompt(q, kb, vb, bias),
        mk_p.reshape(1, n_mem, mem_w), mv_p.reshape(1, n_mem, mem_w),
        lambda u: s5_prompt(u, zeros, zeros, s5p, w_glu_b, b_glu[l], S5_CHUNK))

    def s5_s(u):
        u_tb = u.reshape(bs, ts, ssm_w).transpose(1, 0, 2).reshape(ts * bs, ssm_w)
        o_tb, h_re, h_im = s5_sample(u_tb, state_ssm_re[l].reshape(bs, n_st), state_ssm_im[l].reshape(bs, n_st),
                                     s5p, w_glu_b, b_glu[l], bs, ts)
        return o_tb.reshape(ts, bs, ssm_w).transpose(1, 0, 2).reshape(bs * ts, ssm_w), h_re, h_im

    def attend_s(q, k, v, kb, vb):
        o = sb_attention_sample(q.reshape(bs, ts, sb_w), k.reshape(bs, ts, sb_w), v.reshape(bs, ts, sb_w), bias,
                                cache_k.reshape(-1, PAGE_SIZE * n_heads, SB_HEAD_DIM),
                                cache_v.reshape(-1, PAGE_SIZE * n_heads, SB_HEAD_DIM),
                                page_table + l * cache_k.shape[1])
        return o.reshape(bs * ts, sb_w).astype(BF16)

    y_s, k_s, v_s, hr_s, hi_s = mixer_and_moe(
        x_sample.reshape(bs * ts, d), attend_s,
        cache_mem_k[l].reshape(bs, n_mem, mem_w), cache_mem_v[l].reshape(bs, n_mem, mem_w), s5_s)

    return (y_p.reshape(bp, tp, d), y_s.reshape(bs, ts, d),
            k_p.reshape(1, bp, tp, n_heads, SB_HEAD_DIM), v_p.reshape(1, bp, tp, n_heads, SB_HEAD_DIM),
            k_s.reshape(1, bs, ts, n_heads, SB_HEAD_DIM), v_s.reshape(1, bs, ts, n_heads, SB_HEAD_DIM),
            hr_p.reshape(1, bp, n_groups, n_state), hi_p.reshape(1, bp, n_groups, n_state),
            hr_s.reshape(1, bs, n_groups, n_state), hi_s.reshape(1, bs, n_groups, n_state),
            mk_p.reshape(1, bp, n_mem, MEM_HEADS, mem_dh), mv_p.reshape(1, bp, n_mem, MEM_HEADS, mem_dh))
```

```python
import functools
import math

import jax
import jax.numpy as jnp
from jax import lax
from jax.experimental import pallas as pl
from jax.experimental.pallas import tpu as pltpu

F32 = jnp.float32
BF16 = jnp.bfloat16

V7X_VMEM_BYTES = 64 * 1024 * 1024
VMEM_LIMIT = V7X_VMEM_BYTES - 8 * 1024 * 1024
LANES = 128
SUBLANES = 8

SSM_GROUP = 16
SSM_STATE = 64
SB_HEAD_DIM = 128
MEM_HEADS = 4
PAGE_SIZE = 128
N_GROUPS = 4
EXPERTS_PER_GROUP = 8
N_EXPERTS = N_GROUPS * EXPERTS_PER_GROUP
RMS_EPS = 1e-6

SSM_BLOCK_GROUPS = 16
SSM_BLOCK_CH = SSM_BLOCK_GROUPS * SSM_GROUP
SSM_BLOCK_ST = SSM_BLOCK_GROUPS * SSM_STATE
SCAN_LANES = 512
STRIP = 64
SAMPLE_SPLITS = 1
ROUTER_EXPERT_LANE0 = 32


def _cparams(*sem):
    return pltpu.CompilerParams(dimension_semantics=sem, vmem_limit_bytes=VMEM_LIMIT)


def _sigmoid(x):
    return 1.0 / (1.0 + jnp.exp(-x))


def _tile(m, pref):
    t = min(m, pref)
    assert m % t == 0, (m, pref)
    return t


def _rms(x, g):
    return x * lax.rsqrt(jnp.mean(x * x, axis=-1, keepdims=True) + RMS_EPS) * g


def _rms_kernel(x_ref, g_ref, o_ref):
    o_ref[...] = _rms(x_ref[...], g_ref[...]).astype(o_ref.dtype)


def rmsnorm(x, g, out_dtype, tm=512):
    m, d = x.shape
    tm = _tile(m, tm)
    return pl.pallas_call(
        _rms_kernel,
        grid=(m // tm,),
        in_specs=[pl.BlockSpec((tm, d), lambda i: (i, 0)), pl.BlockSpec((1, d), lambda i: (0, 0))],
        out_specs=pl.BlockSpec((tm, d), lambda i: (i, 0)),
        out_shape=jax.ShapeDtypeStruct((m, d), out_dtype),
        compiler_params=_cparams("parallel"),
        name="rmsnorm",
    )(x, g.reshape(1, d))


def _mm_kernel(*refs, has_res):
    a_ref, b_ref = refs[0], refs[1]
    o_refs = refs[3:] if has_res else refs[2:]
    acc = jnp.dot(a_ref[...], b_ref[...], preferred_element_type=F32)
    if has_res:
        acc = refs[2][...] + acc
    for o_ref in o_refs:
        o_ref[...] = acc.astype(o_ref.dtype)


def matmul(a, b, out_dtypes, res=None, tm=512, tn=1024, name="matmul"):
    m, k = a.shape
    n = b.shape[1]
    tm, tn = _tile(m, tm), _tile(n, tn)
    in_specs = [pl.BlockSpec((tm, k), lambda i, j: (i, 0)), pl.BlockSpec((k, tn), lambda i, j: (0, j))]
    args = [a, b]
    if res is not None:
        in_specs.append(pl.BlockSpec((tm, tn), lambda i, j: (i, j)))
        args.append(res)
    outs = pl.pallas_call(
        functools.partial(_mm_kernel, has_res=res is not None),
        grid=(m // tm, n // tn),
        in_specs=in_specs,
        out_specs=[pl.BlockSpec((tm, tn), lambda i, j: (i, j)) for _ in out_dtypes],
        out_shape=[jax.ShapeDtypeStruct((m, n), dt) for dt in out_dtypes],
        compiler_params=_cparams("parallel", "arbitrary"),
        name=name,
    )(*args)
    return outs


def _proj_kernel(a_ref, b_ref, u_ref, q_ref, k_ref, kb_ref, v_ref, vb_ref, qm_ref, *, q_scale):
    j = pl.program_id(1)
    acc = jnp.dot(a_ref[...], b_ref[...], preferred_element_type=F32)

    @pl.when(j == 0)
    def _():
        u_ref[...] = acc

    @pl.when(j == 1)
    def _():
        q_ref[...] = (acc * q_scale).astype(BF16)

    @pl.when(j == 2)
    def _():
        k_ref[...] = acc
        kb_ref[...] = acc.astype(BF16)

    @pl.when(j == 3)
    def _():
        v_ref[...] = acc
        vb_ref[...] = acc.astype(BF16)

    @pl.when(j == 4)
    def _():
        qm_ref[...] = acc.astype(BF16)


def in_projection(hn, w5, width, tm=512):
    m, k = hn.shape
    tm = _tile(m, tm)
    dts = (F32, BF16, F32, BF16, F32, BF16, BF16)
    return pl.pallas_call(
        functools.partial(_proj_kernel, q_scale=-(SB_HEAD_DIM ** -0.5)),
        grid=(m // tm, 5),
        in_specs=[pl.BlockSpec((tm, k), lambda i, j: (i, 0)), pl.BlockSpec((k, width), lambda i, j: (0, j))],
        out_specs=[pl.BlockSpec((tm, width), lambda i, j: (i, 0)) for _ in dts],
        out_shape=[jax.ShapeDtypeStruct((m, width), dt) for dt in dts],
        compiler_params=_cparams("parallel", "arbitrary"),
        name="in_projection",
    )(hn, w5)


def _cmul_add(ar, ai, hr, hi, xr, xi):
    return ar * hr - ai * hi + xr, ar * hi + ai * hr + xi


def _s5_readout(q, x_ref, u_ref, cblk_ref, d_ref, z_ref):
    cols = slice(q * SSM_BLOCK_CH, (q + 1) * SSM_BLOCK_CH)
    y = jnp.dot(x_ref[...].astype(BF16), cblk_ref[q], preferred_element_type=F32)
    y = y + d_ref[:, cols] * u_ref[:, cols]
    z_ref[:, cols] = jax.nn.gelu(y)


def _s5_glu(z_ref, wglu_ref, bglu_ref):
    z = z_ref[...]
    gate = _sigmoid(jnp.dot(z.astype(BF16), wglu_ref[...], preferred_element_type=F32) + bglu_ref[...])
    return z * gate


def _s5_prompt_kernel(u_ref, h0r_ref, h0i_ref, bblk_ref, cblk_ref, d_ref, apr_ref, api_ref, wglu_ref, bglu_ref,
                      o_ref, hr_out_ref, hi_out_ref, x_ref, z_ref, hcr_ref, hci_ref, *, nj):
    n_blocks = bblk_ref.shape[0]

    @pl.when(pl.program_id(0) == 0)
    def _():
        hcr_ref[...] = h0r_ref[...]
        hci_ref[...] = h0i_ref[...]

    for q in range(n_blocks):
        ub = u_ref[:, q * SSM_BLOCK_CH:(q + 1) * SSM_BLOCK_CH]
        x_ref[...] = jnp.dot(ub.astype(BF16), bblk_ref[q], preferred_element_type=F32)
        for c in range(SSM_BLOCK_ST // SCAN_LANES):
            re = slice(c * SCAN_LANES, (c + 1) * SCAN_LANES)
            im = slice(SSM_BLOCK_ST + c * SCAN_LANES, SSM_BLOCK_ST + (c + 1) * SCAN_LANES)
            st = slice(q * SSM_BLOCK_ST + c * SCAN_LANES, q * SSM_BLOCK_ST + (c + 1) * SCAN_LANES)
            ar = jnp.broadcast_to(apr_ref[0:1, st], (SUBLANES, SCAN_LANES))
            ai = jnp.broadcast_to(api_ref[0:1, st], (SUBLANES, SCAN_LANES))

            def scan_body(j, carry, re=re, im=im, ar=ar, ai=ai):
                rows = pl.ds(pl.multiple_of(j * SUBLANES, SUBLANES), SUBLANES)
                hr, hi = _cmul_add(ar, ai, carry[0], carry[1], x_ref[rows, re], x_ref[rows, im])
                x_ref[rows, re] = hr
                x_ref[rows, im] = hi
                return hr, hi

            zero = jnp.zeros((SUBLANES, SCAN_LANES), F32)
            er, ei = lax.fori_loop(0, nj, scan_body, (zero, zero), unroll=2)

            alr, ali = apr_ref[nj - 1:nj, st], api_ref[nj - 1:nj, st]
            cr, ci = hcr_ref[:, st], hci_ref[:, st]
            crs, cis = [], []
            for s in range(SUBLANES):
                crs.append(cr)
                cis.append(ci)
                cr, ci = _cmul_add(alr, ali, cr, ci, er[s:s + 1, :], ei[s:s + 1, :])
            hcr_ref[:, st] = cr
            hci_ref[:, st] = ci
            cin_r = jnp.concatenate(crs, axis=0)
            cin_i = jnp.concatenate(cis, axis=0)

            def fix_body(j, _, re=re, im=im, st=st, cin_r=cin_r, cin_i=cin_i):
                rows = pl.ds(pl.multiple_of(j * SUBLANES, SUBLANES), SUBLANES)
                pr = jnp.broadcast_to(apr_ref[pl.ds(j, 1), st], (SUBLANES, SCAN_LANES))
                pi = jnp.broadcast_to(api_ref[pl.ds(j, 1), st], (SUBLANES, SCAN_LANES))
                hr, hi = _cmul_add(pr, pi, cin_r, cin_i, x_ref[rows, re], x_ref[rows, im])
                x_ref[rows, re] = hr
                x_ref[rows, im] = hi
                return 0

            lax.fori_loop(0, nj, fix_body, 0, unroll=2)
        _s5_readout(q, x_ref, u_ref, cblk_ref, d_ref, z_ref)

    o_ref[...] = _s5_glu(z_ref, wglu_ref, bglu_ref).astype(o_ref.dtype)
    hr_out_ref[...] = hcr_ref[...]
    hi_out_ref[...] = hci_ref[...]


def _s5_sample_kernel(u_ref, h0r_ref, h0i_ref, bblk_ref, cblk_ref, d_ref, apr_ref, api_ref, wglu_ref, bglu_ref,
                      o_ref, hr_out_ref, hi_out_ref, x_ref, z_ref, *, nb, nt):
    n_blocks = bblk_ref.shape[0]
    for q in range(n_blocks):
        ub = u_ref[:, q * SSM_BLOCK_CH:(q + 1) * SSM_BLOCK_CH]
        x_ref[...] = jnp.dot(ub.astype(BF16), bblk_ref[q], preferred_element_type=F32)
        for c in range(SSM_BLOCK_ST // SCAN_LANES):
            re = slice(c * SCAN_LANES, (c + 1) * SCAN_LANES)
            im = slice(SSM_BLOCK_ST + c * SCAN_LANES, SSM_BLOCK_ST + (c + 1) * SCAN_LANES)
            st = slice(q * SSM_BLOCK_ST + c * SCAN_LANES, q * SSM_BLOCK_ST + (c + 1) * SCAN_LANES)
            ar, ai = apr_ref[0:1, st], api_ref[0:1, st]
            hr, hi = h0r_ref[:, st], h0i_ref[:, st]
            for t in range(nt):
                rows = slice(t * nb, (t + 1) * nb)
                hr, hi = _cmul_add(ar, ai, hr, hi, x_ref[rows, re], x_ref[rows, im])
                x_ref[rows, re] = hr
                x_ref[rows, im] = hi
            hr_out_ref[:, st] = hr
            hi_out_ref[:, st] = hi
        _s5_readout(q, x_ref, u_ref, cblk_ref, d_ref, z_ref)
    o_ref[...] = _s5_glu(z_ref, wglu_ref, bglu_ref).astype(o_ref.dtype)


def _s5_params(lam_re, lam_im, log_step, b_re, b_im, c_re, c_im, d_skip, n_pow):
    g, p = lam_re.shape
    dt = jnp.exp(log_step)[:, None]
    steps = jnp.arange(1, n_pow + 1, dtype=F32)[:, None, None]
    mag = jnp.exp(lam_re * dt * steps)
    phase = lam_im * dt * steps
    ap_re = (mag * jnp.cos(phase)).reshape(n_pow, g * p)
    ap_im = (mag * jnp.sin(phase)).reshape(n_pow, g * p)
    a_re, a_im = mag[0] * jnp.cos(phase[0]), mag[0] * jnp.sin(phase[0])
    num_re = a_re - 1.0
    den = lam_re * lam_re + lam_im * lam_im
    k_re = (num_re * lam_re + a_im * lam_im) / den
    k_im = (a_im * lam_re - num_re * lam_im) / den
    bp_re = k_re[..., None] * b_re - k_im[..., None] * b_im
    bp_im = k_re[..., None] * b_im + k_im[..., None] * b_re
    nblk = g // SSM_BLOCK_GROUPS
    eye = jnp.eye(SSM_BLOCK_GROUPS, dtype=F32)

    def b_block(bp):
        bp = bp.reshape(nblk, SSM_BLOCK_GROUPS, p, SSM_GROUP)
        return jnp.einsum('qgpc,gh->qgchp', bp, eye).reshape(nblk, SSM_BLOCK_CH, SSM_BLOCK_ST)

    def c_block(c):
        c = c.reshape(nblk, SSM_BLOCK_GROUPS, SSM_GROUP, p)
        return jnp.einsum('qgcp,gh->qgphc', c, eye).reshape(nblk, SSM_BLOCK_ST, SSM_BLOCK_CH)

    bblk = jnp.concatenate([b_block(bp_re), b_block(bp_im)], axis=2).astype(BF16)
    cblk = jnp.concatenate([c_block(c_re), -c_block(c_im)], axis=1).astype(BF16)
    return bblk, cblk, d_skip.reshape(1, g * SSM_GROUP), ap_re, ap_im


def _const_spec(shape):
    return pl.BlockSpec(shape, lambda *_: (0,) * len(shape))


def s5_prompt(u, h0r, h0i, params, w_glu, b_glu, chunk):
    t, w = u.shape
    nj = chunk // SUBLANES
    bblk, cblk, d, apr, api = params
    ns = apr.shape[1]
    consts = [h0r, h0i, bblk, cblk, d, apr, api, w_glu, b_glu.reshape(1, w)]
    u_perm = u.reshape(t // chunk, SUBLANES, nj, w).transpose(0, 2, 1, 3).reshape(t, w)
    o_perm, h_re, h_im = pl.pallas_call(
        functools.partial(_s5_prompt_kernel, nj=nj),
        grid=(t // chunk,),
        in_specs=[pl.BlockSpec((chunk, w), lambda i: (i, 0))] + [_const_spec(c.shape) for c in consts],
        out_specs=[pl.BlockSpec((chunk, w), lambda i: (i, 0)), _const_spec((1, ns)), _const_spec((1, ns))],
        out_shape=[jax.ShapeDtypeStruct((t, w), BF16), jax.ShapeDtypeStruct((1, ns), F32),
                   jax.ShapeDtypeStruct((1, ns), F32)],
        scratch_shapes=[pltpu.VMEM((chunk, 2 * SSM_BLOCK_ST), F32), pltpu.VMEM((chunk, w), F32),
                        pltpu.VMEM((1, ns), F32), pltpu.VMEM((1, ns), F32)],
        compiler_params=_cparams("arbitrary"),
        name="s5_prompt",
    )(u_perm, *consts)
    o = o_perm.reshape(t // chunk, nj, SUBLANES, w).transpose(0, 2, 1, 3).reshape(t, w)
    return o, h_re, h_im


def s5_sample(u_tb, h0r, h0i, params, w_glu, b_glu, nb, nt):
    m, w = u_tb.shape
    bblk, cblk, d, apr, api = params
    ns = apr.shape[1]
    consts = [h0r, h0i, bblk, cblk, d, apr, api, w_glu, b_glu.reshape(1, w)]
    return pl.pallas_call(
        functools.partial(_s5_sample_kernel, nb=nb, nt=nt),
        grid=(1,),
        in_specs=[_const_spec(u_tb.shape)] + [_const_spec(c.shape) for c in consts],
        out_specs=[_const_spec((m, w)), _const_spec((nb, ns)), _const_spec((nb, ns))],
        out_shape=[jax.ShapeDtypeStruct((m, w), BF16), jax.ShapeDtypeStruct((nb, ns), F32),
                   jax.ShapeDtypeStruct((nb, ns), F32)],
        scratch_shapes=[pltpu.VMEM((m, 2 * SSM_BLOCK_ST), F32), pltpu.VMEM((m, w), F32)],
        compiler_params=_cparams("arbitrary"),
        name="s5_sample",
    )(u_tb, *consts)


def _log_sigmoid_pair(w):
    sign_bit = jnp.uint32(0x80000000)
    neg_abs = lax.bitcast_convert_type(lax.bitcast_convert_type(w, jnp.uint32) | sign_bit, F32)
    lk = jnp.minimum(w, 0.0) - jnp.log(1.0 + jnp.exp(neg_abs))
    return lk - w, lk


def _split3_bf16(x):
    hi = x.astype(BF16).astype(F32)
    mid = (x - hi).astype(BF16).astype(F32)
    lo = (x - hi - mid).astype(BF16).astype(F32)
    return hi, mid, lo


def _sb_prompt_kernel(bias_ref, q_ref, k_ref, v_ref, o_ref, acc_ref, carry_ref, w_ref, *, tq, tk, n_hp):
    hp, i = pl.program_id(0), pl.program_id(1)
    kb_per_q = tq // tk
    row = lax.broadcasted_iota(jnp.int32, (tq, tk), 0)
    col = lax.broadcasted_iota(jnp.int32, (tq, tk), 1)
    newer = (lax.broadcasted_iota(jnp.int32, (tk, tk), 0)
             > lax.broadcasted_iota(jnp.int32, (tk, tk), 1)).astype(BF16)
    acc_ref[...] = jnp.zeros_like(acc_ref)
    carry_ref[...] = jnp.zeros_like(carry_ref)

    lane_q = lax.broadcasted_iota(jnp.int32, (tq, SB_HEAD_DIM), 1)
    lane_k = lax.broadcasted_iota(jnp.int32, (tk, SB_HEAD_DIM), 1)
    q_units = jnp.where(lane_q < 3, 1.0, 0.0).astype(BF16)
    q_aug, k_bias = [], []
    for hh in range(n_hp):
        hi, mid, lo = _split3_bf16(jnp.full((tk, SB_HEAD_DIM), -bias_ref[hp * n_hp + hh], F32))
        k_bias.append(jnp.where(lane_k == 0, hi, jnp.where(lane_k == 1, mid, jnp.where(lane_k == 2, lo, 0.0)))
                      .astype(BF16))
        q_aug.append(jnp.concatenate([q_ref[:, hh * SB_HEAD_DIM:(hh + 1) * SB_HEAD_DIM], q_units], axis=1))

    def logits(kb, hh):
        rows = pl.ds(pl.multiple_of(kb * tk, tk), tk)
        k_aug = jnp.concatenate([k_ref[rows, hh * SB_HEAD_DIM:(hh + 1) * SB_HEAD_DIM], k_bias[hh]], axis=1)
        return lax.dot_general(q_aug[hh], k_aug, (((1,), (1,)), ((), ())), preferred_element_type=F32)

    def weigh(kb, hh, w, mask):
        rows = pl.ds(pl.multiple_of(kb * tk, tk), tk)
        ls_s, lk_s, lkb_s = [], [], []
        for s in range(tq // STRIP):
            r = slice(s * STRIP, (s + 1) * STRIP)
            ls, lk = _log_sigmoid_pair(w[r])
            if mask is not None:
                lk = jnp.where(mask[r], lk, 0.0)
            ls_s.append(ls)
            lk_s.append(jnp.sum(lk, axis=1, keepdims=True))
            lkb_s.append(lk.astype(BF16))
        cum = jnp.dot(jnp.concatenate(lkb_s, axis=0), newer, preferred_element_type=F32)
        a_s = []
        for s in range(tq // STRIP):
            r = slice(s * STRIP, (s + 1) * STRIP)
            a = jnp.exp(ls_s[s] + cum[r] + carry_ref[hh, r, :])
            if mask is not None:
                a = jnp.where(mask[r], a, 0.0)
            a_s.append(a.astype(BF16))
        acc_ref[hh] += jnp.dot(jnp.concatenate(a_s, axis=0), v_ref[rows, hh * SB_HEAD_DIM:(hh + 1) * SB_HEAD_DIM],
                               preferred_element_type=F32)
        carry_ref[hh] += jnp.concatenate(lk_s, axis=0)

    def tile(kb, mask):
        for hh in range(n_hp):
            weigh(kb, hh, logits(kb, hh), mask)

    for d in reversed(range(kb_per_q)):
        tile(i * kb_per_q + d, col + d * tk < row)

    for hh in range(n_hp):
        w_ref[hh] = logits(jnp.maximum(i * kb_per_q - 1, 0), hh)

    def body(jj, _):
        base = (i - jj) * kb_per_q
        for hh in range(n_hp):
            weigh(base - 1, hh, w_ref[hh], None)
        for d in range(1, kb_per_q):
            tile(base - 1 - d, None)
        for hh in range(n_hp):
            w_ref[hh] = logits(jnp.maximum(base - kb_per_q - 1, 0), hh)
        return 0

    lax.fori_loop(0, i, body, 0)
    for hh in range(n_hp):
        o_ref[:, hh * SB_HEAD_DIM:(hh + 1) * SB_HEAD_DIM] = acc_ref[hh].astype(o_ref.dtype)


def sb_attention_prompt(q, k, v, bias, tq=512, tk=256, heads_per_step=2):
    t, w = q.shape
    n_heads = w // SB_HEAD_DIM
    tq, tk, n_hp = _tile(t, tq), _tile(t, tk), heads_per_step
    assert tq % tk == 0 and n_heads % n_hp == 0
    wb = n_hp * SB_HEAD_DIM
    grid_spec = pltpu.PrefetchScalarGridSpec(
        num_scalar_prefetch=1,
        grid=(n_heads // n_hp, t // tq),
        in_specs=[pl.BlockSpec((tq, wb), lambda h, i, b: (i, h)),
                  pl.BlockSpec((t, wb), lambda h, i, b: (0, h)),
                  pl.BlockSpec((t, wb), lambda h, i, b: (0, h))],
        out_specs=pl.BlockSpec((tq, wb), lambda h, i, b: (i, h)),
        scratch_shapes=[pltpu.VMEM((n_hp, tq, SB_HEAD_DIM), F32), pltpu.VMEM((n_hp, tq, 1), F32),
                        pltpu.VMEM((n_hp, tq, tk), F32)],
    )
    return pl.pallas_call(
        functools.partial(_sb_prompt_kernel, tq=tq, tk=tk, n_hp=n_hp),
        grid_spec=grid_spec,
        out_shape=jax.ShapeDtypeStruct((t, w), BF16),
        compiler_params=_cparams("parallel", "arbitrary"),
        name="sb_attention_prompt",
    )(bias, q, k, v)


def _sb_sample_kernel(pt_ref, *refs, n_pp, n_heads):
    wq_ref, bias_ref, kn_ref, vn_ref = refs[:4]
    kp_refs = refs[4:4 + n_pp]
    vp_refs = refs[4 + n_pp:4 + 2 * n_pp]
    o_ref, acc_ref, carry_ref = refs[4 + 2 * n_pp:]
    p = pl.program_id(1)
    tk = PAGE_SIZE
    nq = o_ref.shape[0]
    dh = SB_HEAD_DIM
    newer = (lax.broadcasted_iota(jnp.int32, (tk, tk), 1)
             > lax.broadcasted_iota(jnp.int32, (tk, tk), 0)).astype(BF16)

    def head_rows(ref, h):
        return ref[pl.ds(h, tk, stride=n_heads), :]

    def weights(z, mask, carry):
        ls, lk = _log_sigmoid_pair(z)
        if mask is not None:
            lk = jnp.where(mask, lk, 0.0)
        after = jnp.dot(newer, lk.astype(BF16), preferred_element_type=F32) + carry
        a = jnp.exp(ls + after)
        if mask is not None:
            a = jnp.where(mask, a, 0.0)
        return a.T, jnp.sum(lk, axis=0, keepdims=True)

    @pl.when(p == 0)
    def _():
        lane_q = lax.broadcasted_iota(jnp.int32, (tk, LANES), 1) % nq
        key = lax.broadcasted_iota(jnp.int32, (tk, LANES), 0)
        fill = jnp.zeros((tk - nq, kn_ref.shape[1]), F32)
        kn = jnp.concatenate([kn_ref[...], fill], axis=0).astype(BF16)
        vn = jnp.concatenate([vn_ref[...], fill], axis=0).astype(BF16)
        z = jnp.dot(kn, wq_ref[...], preferred_element_type=F32) - bias_ref[...]
        at, tot = weights(z, key < lane_q, jnp.zeros((1, LANES), F32))
        carry_ref[...] = tot
        for h in range(n_heads):
            acc_ref[h * nq:(h + 1) * nq, :] = jnp.dot(at[h * nq:(h + 1) * nq, :].astype(BF16),
                                                      vn[:, h * dh:(h + 1) * dh], preferred_element_type=F32)

    per = n_pp // SAMPLE_SPLITS
    carry = carry_ref[...]
    partial = [None] * n_heads
    for sp in range(SAMPLE_SPLITS):
        pages = list(range(sp * per, (sp + 1) * per))
        z = -bias_ref[...]
        for hp in range(n_heads // 2):
            lhs = jnp.concatenate(
                [jnp.concatenate([head_rows(kp_refs[r], 2 * hp), head_rows(kp_refs[r], 2 * hp + 1)], axis=1)
                 for r in pages], axis=0).astype(BF16)
            z = z + jnp.dot(lhs, wq_ref[2 * hp * dh:(2 * hp + 2) * dh, :], preferred_element_type=F32)
        ats = []
        for idx in range(per):
            at, tot = weights(z[idx * tk:(idx + 1) * tk, :], None, carry)
            carry = carry + tot
            ats.append(at)
        for h in range(n_heads):
            lhs = jnp.concatenate([at[h * nq:(h + 1) * nq, :] for at in ats], axis=1).astype(BF16)
            rhs = jnp.concatenate([head_rows(vp_refs[r], h) for r in pages], axis=0).astype(BF16)
            term = jnp.dot(lhs, rhs, preferred_element_type=F32)
            partial[h] = term if partial[h] is None else partial[h] + term
    carry_ref[...] = carry
    for h in range(n_heads):
        acc_ref[h * nq:(h + 1) * nq, :] += partial[h]

    @pl.when(p == pl.num_programs(1) - 1)
    def _():
        for h in range(n_heads):
            o_ref[:, h * dh:(h + 1) * dh] = acc_ref[h * nq:(h + 1) * nq, :]


def sb_attention_sample(q, k_new, v_new, bias, cache_k, cache_v, page_table, pages_per_step=8):
    nb, nq, w = q.shape
    n_heads = w // SB_HEAD_DIM
    n_pages = page_table.shape[1]
    n_pp = pages_per_step
    assert n_pages % n_pp == 0 and n_heads * nq <= LANES and nq <= PAGE_SIZE and n_heads % 2 == 0
    assert LANES % nq == 0 and nq % SUBLANES == 0
    q4 = q.reshape(nb, nq, n_heads, SB_HEAD_DIM).transpose(0, 2, 3, 1)
    same_head = jnp.arange(n_heads)[:, None] == jnp.arange(LANES // nq)[None, :]
    wq = jnp.where(same_head[None, :, None, :, None], q4[:, :, :, None, :], 0).reshape(nb, w, LANES)
    bias_row = jnp.pad(jnp.repeat(bias, nq), (0, LANES - n_heads * nq)).reshape(1, LANES)

    page_rows = PAGE_SIZE * n_heads

    def page_spec(r):
        return pl.BlockSpec((None, page_rows, SB_HEAD_DIM),
                            lambda b, p, pt: (pt[b, n_pages - 1 - (p * n_pp + r)], 0, 0))

    per_b = lambda b, p, pt: (b, 0, 0)
    grid_spec = pltpu.PrefetchScalarGridSpec(
        num_scalar_prefetch=1,
        grid=(nb, n_pages // n_pp),
        in_specs=[pl.BlockSpec((None, w, LANES), per_b), pl.BlockSpec((1, LANES), lambda b, p, pt: (0, 0)),
                  pl.BlockSpec((None, nq, w), per_b), pl.BlockSpec((None, nq, w), per_b)]
                 + [page_spec(r) for r in range(n_pp)] + [page_spec(r) for r in range(n_pp)],
        out_specs=pl.BlockSpec((None, nq, w), per_b),
        scratch_shapes=[pltpu.VMEM((n_heads * nq, SB_HEAD_DIM), F32), pltpu.VMEM((1, LANES), F32)],
    )
    return pl.pallas_call(
        functools.partial(_sb_sample_kernel, n_pp=n_pp, n_heads=n_heads),
        grid_spec=grid_spec,
        out_shape=jax.ShapeDtypeStruct((nb, nq, w), F32),
        compiler_params=_cparams("parallel", "arbitrary"),
        name="sb_attention_sample",
    )(page_table, wq, bias_row, k_new, v_new, *([cache_k] * n_pp), *([cache_v] * n_pp))


def _mem_attn_kernel(q_ref, k_ref, v_ref, o_ref, *, scale):
    dh = o_ref.shape[1] // MEM_HEADS
    for h in range(MEM_HEADS):
        cols = slice(h * dh, (h + 1) * dh)
        kh = k_ref[:, cols].astype(BF16)
        vh = v_ref[:, cols].astype(BF16)
        s = lax.dot_general(q_ref[:, cols], kh, (((1,), (1,)), ((), ())), preferred_element_type=F32) * scale
        e = jnp.exp(s - jnp.max(s, axis=-1, keepdims=True))
        p = e / jnp.sum(e, axis=-1, keepdims=True)
        o_ref[:, cols] = jnp.dot(p.astype(BF16), vh, preferred_element_type=F32).astype(o_ref.dtype)


def memory_attention(q, mem_k, mem_v, tq=512):
    nb, t, w = q.shape
    n_mem = mem_k.shape[1]
    tq = _tile(t, tq)
    return pl.pallas_call(
        functools.partial(_mem_attn_kernel, scale=(w // MEM_HEADS) ** -0.5),
        grid=(nb, t // tq),
        in_specs=[pl.BlockSpec((None, tq, w), lambda b, i: (b, i, 0)),
                  pl.BlockSpec((None, n_mem, w), lambda b, i: (b, 0, 0)),
                  pl.BlockSpec((None, n_mem, w), lambda b, i: (b, 0, 0))],
        out_specs=pl.BlockSpec((None, tq, w), lambda b, i: (b, i, 0)),
        out_shape=jax.ShapeDtypeStruct((nb, t, w), BF16),
        compiler_params=_cparams("parallel", "arbitrary"),
        name="memory_attention",
    )(q, mem_k, mem_v)


def _merge_kernel(hn_ref, o0_ref, o1_ref, o2_ref, g0_ref, g1_ref, g2_ref, b0_ref, b1_ref, b2_ref, out_ref):
    hn = hn_ref[...]
    total = None
    for o_ref, g_ref, b_ref in ((o0_ref, g0_ref, b0_ref), (o1_ref, g1_ref, b1_ref), (o2_ref, g2_ref, b2_ref)):
        gate = _sigmoid(jnp.dot(hn, g_ref[...], preferred_element_type=F32))
        term = gate * jnp.dot(o_ref[...], b_ref[...], preferred_element_type=F32)
        total = term if total is None else total + term
    out_ref[...] = total.astype(out_ref.dtype)


def gated_merge(hn, branches, w_gates, w_branches, tm=512, tn=512):
    m, d = hn.shape
    tm, tn = _tile(m, tm), _tile(d, tn)
    nj = d // tn
    row = lambda i, j: (i, 0)
    in_specs = [pl.BlockSpec((tm, d), row)]
    in_specs += [pl.BlockSpec((tm, o.shape[1]), row) for o in branches]
    in_specs += [pl.BlockSpec((d, tn), functools.partial(lambda i, j, b: (0, b * nj + j), b=b)) for b in range(3)]
    in_specs += [pl.BlockSpec((w.shape[0], tn), lambda i, j: (0, j)) for w in w_branches]
    return pl.pallas_call(
        _merge_kernel,
        grid=(m // tm, nj),
        in_specs=in_specs,
        out_specs=pl.BlockSpec((tm, tn), lambda i, j: (i, j)),
        out_shape=jax.ShapeDtypeStruct((m, d), BF16),
        compiler_params=_cparams("parallel", "arbitrary"),
        name="gated_merge",
    )(hn, *branches, w_gates, w_gates, w_gates, *w_branches)


MOE_TILE = 1024
MOE_ROWS = 320
MOE_EXPERTS_PER_STEP = 2
ROUTER_GROUP_LANE = 64
ROUTER_RANK_LANE = 65


def _router_kernel(x_ref, g_ref, wr_ref, br_ref, hn_ref, gate_ref, cnt_ref, run_ref):
    hn = _rms(x_ref[...], g_ref[...])
    hn_ref[...] = hn.astype(hn_ref.dtype)
    logits = jnp.dot(hn, wr_ref[...], preferred_element_type=F32, precision=lax.Precision.HIGHEST) + br_ref[...]
    lane = lax.broadcasted_iota(jnp.int32, logits.shape, 1).astype(F32)
    neg, far = -1e30, 1e9
    rmax = lambda v: jnp.max(v, axis=-1, keepdims=True)
    rmin = lambda v: jnp.min(v, axis=-1, keepdims=True)
    rsum = lambda v: jnp.sum(v, axis=-1, keepdims=True)
    is_group = lane < N_GROUPS
    gl = jnp.where(is_group, logits, neg)
    gmax = rmax(gl)
    g_val = 1.0 / rsum(jnp.where(is_group, jnp.exp(gl - gmax), 0.0))
    g_idx = rmin(jnp.where(gl == gmax, lane, far))
    lo = ROUTER_EXPERT_LANE0 + EXPERTS_PER_GROUP * g_idx
    in_group = jnp.logical_and(lane >= lo, lane < lo + EXPERTS_PER_GROUP)
    el = jnp.where(in_group, logits, neg)
    ex = jnp.where(in_group, jnp.exp(el - rmax(el)), 0.0)
    prob = jnp.where(in_group, ex / rsum(ex), -1.0)
    v1 = rmax(prob)
    i1 = rmin(jnp.where(prob == v1, lane, far))
    rest = jnp.where(lane == i1, -1.0, prob)
    v2 = rmax(rest)
    i2 = rmin(jnp.where(rest == v2, lane, far))
    within = jnp.where(lane == i1, v1 / (v1 + v2), jnp.where(lane == i2, v2 / (v1 + v2), 0.0))

    @pl.when(pl.program_id(1) == 0)
    def _():
        run_ref[...] = jnp.zeros_like(run_ref)

    tm = logits.shape[0]
    chose = jnp.where(lane == g_idx, 1.0, 0.0)
    earlier = (lax.broadcasted_iota(jnp.int32, (tm, tm), 1)
               < lax.broadcasted_iota(jnp.int32, (tm, tm), 0)).astype(BF16)
    before = jnp.dot(earlier, chose.astype(BF16), preferred_element_type=F32) + run_ref[...]
    rank = rsum(chose * before)
    run_ref[...] += jnp.sum(chose, axis=0, keepdims=True)
    cnt_ref[...] = run_ref[...]
    gate_ref[...] = (g_val * within + jnp.where(lane == ROUTER_GROUP_LANE, g_idx, 0.0)
                     + jnp.where(lane == ROUTER_RANK_LANE, rank, 0.0))


def moe_router(x, g, w_router, b_router, tile, tm=256):
    m, d = x.shape
    tm = _tile(tile, tm)
    sub = tile // tm
    row = lambda i, s: (i * sub + s, 0)
    return pl.pallas_call(
        _router_kernel,
        grid=(m // tile, sub),
        in_specs=[pl.BlockSpec((tm, d), row), _const_spec((1, d)), _const_spec((d, LANES)), _const_spec((1, LANES))],
        out_specs=[pl.BlockSpec((tm, d), row), pl.BlockSpec((tm, LANES), row),
                   pl.BlockSpec((None, 1, LANES), lambda i, s: (i, 0, 0))],
        out_shape=[jax.ShapeDtypeStruct((m, d), BF16), jax.ShapeDtypeStruct((m, LANES), F32),
                   jax.ShapeDtypeStruct((m // tile, 1, LANES), F32)],
        scratch_shapes=[pltpu.VMEM((1, LANES), F32)],
        compiler_params=_cparams("parallel", "arbitrary"),
        name="moe_router",
    )(x, g.reshape(1, d), w_router, b_router)


def _experts_kernel(bg_ref, nb_ref, hn_ref, gate_ref, x_ref, drow_ref, dcol_ref, wgu_ref, wd_ref, gf_ref, o_ref,
                    xs_ref, gs_ref, yb_ref):
    i, b, j = pl.program_id(0), pl.program_id(1), pl.program_id(2)
    tile = hn_ref.shape[0]
    rows = xs_ref.shape[0]
    ff = wd_ref.shape[1]
    active = b < nb_ref[i]
    last_j = j == pl.num_programs(2) - 1

    @pl.when(jnp.logical_and(b == 0, j == 0))
    def _():
        o_ref[...] = x_ref[...]

    @pl.when(jnp.logical_and(active, j == 0))
    def _():
        slot = lax.broadcasted_iota(jnp.int32, (rows, tile), 0) + b * rows
        pick = jnp.where(drow_ref[...] == slot, 1.0, 0.0).astype(BF16)
        xs_ref[...] = jnp.dot(pick, hn_ref[...], preferred_element_type=F32).astype(BF16)
        gs = None
        for part in _split3_bf16(gate_ref[...]):
            term = jnp.dot(pick, part.astype(BF16), preferred_element_type=F32)
            gs = term if gs is None else gs + term
        gs_ref[...] = gs
        yb_ref[...] = jnp.zeros_like(yb_ref)

    @pl.when(active)
    def _():
        lane = lax.broadcasted_iota(jnp.int32, gs_ref.shape, 1)
        y = yb_ref[...]
        for k in range(MOE_EXPERTS_PER_STEP):
            e_lane = ROUTER_EXPERT_LANE0 + bg_ref[i, b] * EXPERTS_PER_GROUP + j * MOE_EXPERTS_PER_STEP + k
            gate = jnp.sum(jnp.where(lane == e_lane, gs_ref[...], 0.0), axis=-1, keepdims=True)
            au = jnp.dot(xs_ref[...], wgu_ref[k], preferred_element_type=F32)
            a, up = au[:, :ff], au[:, ff:]
            act = a * _sigmoid(a) * up * gate
            y = y + jnp.dot(act.astype(BF16), wd_ref[k], preferred_element_type=F32)
        yb_ref[...] = y

    @pl.when(jnp.logical_and(active, last_j))
    def _():
        slot = lax.broadcasted_iota(jnp.int32, (tile, rows), 1) + b * rows
        put = jnp.where(dcol_ref[...] == slot, 1.0, 0.0).astype(BF16)
        o_ref[...] += jnp.dot(put, yb_ref[...].astype(BF16), preferred_element_type=F32)

    @pl.when(jnp.logical_and(b == pl.num_programs(1) - 1, last_j))
    def _():
        o_ref[...] = _rms(o_ref[...], gf_ref[...])


def moe_experts_final_norm(hn, route, counts, x, w_gate_up, w_down, g_final, tile):
    m, d = x.shape
    n_exp, _, ff2 = w_gate_up.shape
    n_tiles = m // tile
    rows = min(MOE_ROWS, tile)
    n_blocks = (tile + N_GROUPS * (rows - 1)) // rows
    per = MOE_EXPERTS_PER_STEP
    steps = EXPERTS_PER_GROUP // per
    w_gate_up = w_gate_up.reshape(n_exp // per, per, d, ff2)
    w_down = w_down.reshape(n_exp // per, per, ff2 // 2, d)
    cnt = counts[:, 0, :N_GROUPS].astype(jnp.int32)
    blocks_g = (cnt + rows - 1) // rows
    end_g = jnp.cumsum(blocks_g, axis=1)
    start_g = end_g - blocks_g
    n_blk = end_g[:, -1]
    bidx = jnp.arange(n_blocks, dtype=jnp.int32)[None, :]
    blk_group = jnp.sum(bidx[:, :, None] >= end_g[:, None, :], axis=2).astype(jnp.int32)
    last_group = jnp.sum(jnp.where(bidx == jnp.maximum(n_blk - 1, 0)[:, None], blk_group, 0), axis=1, keepdims=True)
    blk_group = jnp.minimum(blk_group, last_group)
    grp = route[:, ROUTER_GROUP_LANE].astype(jnp.int32).reshape(n_tiles, tile)
    rank = route[:, ROUTER_RANK_LANE].astype(jnp.int32).reshape(n_tiles, tile)
    start_t = sum(jnp.where(grp == g, start_g[:, g:g + 1], 0) for g in range(N_GROUPS))
    dest = rows * start_t + rank

    def w_index(i, b, j, bg, nb):
        live = b < nb[i]
        last = jnp.maximum(nb[i] - 1, 0)
        return (jnp.where(live, bg[i, b] * steps + j, bg[i, last] * steps + steps - 1), 0, 0, 0)

    row = lambda i, b, j, bg, nb: (i, 0)
    grid_spec = pltpu.PrefetchScalarGridSpec(
        num_scalar_prefetch=2,
        grid=(n_tiles, n_blocks, steps),
        in_specs=[pl.BlockSpec((tile, d), row, pipeline_mode=pl.Buffered(1)), pl.BlockSpec((tile, LANES), row),
                  pl.BlockSpec((tile, d), row, pipeline_mode=pl.Buffered(1)),
                  pl.BlockSpec((None, 1, tile), lambda i, b, j, bg, nb: (i, 0, 0)),
                  pl.BlockSpec((tile, 1), row),
                  pl.BlockSpec((None, per, d, ff2), w_index), pl.BlockSpec((None, per, ff2 // 2, d), w_index),
                  pl.BlockSpec((1, d), lambda i, b, j, bg, nb: (0, 0))],
        out_specs=pl.BlockSpec((tile, d), row),
        scratch_shapes=[pltpu.VMEM((rows, d), BF16), pltpu.VMEM((rows, LANES), F32), pltpu.VMEM((rows, d), F32)],
    )
    return pl.pallas_call(
        _experts_kernel,
        grid_spec=grid_spec,
        out_shape=jax.ShapeDtypeStruct((m, d), F32),
        compiler_params=_cparams("parallel", "arbitrary", "arbitrary"),
        name="moe_experts",
    )(blk_group, n_blk, hn, route, x, dest.reshape(n_tiles, 1, tile), dest.reshape(m, 1), w_gate_up, w_down,
      g_final.reshape(1, d))


S5_CHUNK = 512


def kernel(x_prompt, x_sample, mem_prompt, cache_k, cache_v, page_table, cache_mem_k, cache_mem_v, state_ssm_re, state_ssm_im, g_mix_norm, w_in, sb_logit_bias, ssm_lambda_re, ssm_lambda_im, ssm_log_step, ssm_b_re, ssm_b_im, ssm_c_re, ssm_c_im, ssm_d, w_glu, b_glu, g_mem_norm, w_mem_k, w_mem_v, w_branch_ssm, w_branch_sb, w_branch_mem, w_out, g_ffn_norm, w_group_router, b_group_router, w_expert_router, b_expert_router, w_exp_gate, w_exp_up, w_exp_down, g_final):
    depth = w_in.shape[0]
    assert depth == 1, "one mixer + MoE layer per step"
    bp, tp, d = x_prompt.shape
    bs, ts, _ = x_sample.shape
    assert bp == 1
    n_groups, n_state = ssm_lambda_re.shape[1:]
    ssm_w = n_groups * SSM_GROUP
    n_st = n_groups * n_state
    sb_w = cache_k.shape[3] * cache_k.shape[4]
    mem_w = cache_mem_k.shape[3] * cache_mem_k.shape[4]
    n_mem = mem_prompt.shape[1]
    assert ssm_w == sb_w == mem_w
    n_heads = sb_w // SB_HEAD_DIM
    mem_dh = mem_w // MEM_HEADS
    l = 0

    w5 = w_in[l, :, :5 * ssm_w].astype(BF16)
    w_gates = w_in[l, :, 5 * ssm_w:].astype(BF16)
    w_branches = [w_branch_ssm[l].astype(BF16), w_branch_sb[l].astype(BF16), w_branch_mem[l].astype(BF16)]
    w_out_b = w_out[l].astype(BF16)
    w_glu_b = w_glu[l].astype(BF16)
    w_memkv = jnp.concatenate([w_mem_k[l], w_mem_v[l]], axis=1).astype(BF16)
    w_gate_up = jnp.concatenate([w_exp_gate[l], w_exp_up[l]], axis=2).astype(BF16)
    w_down = w_exp_down[l].astype(BF16)
    def router_lanes(group_part, expert_part):
        rows = group_part.shape[0]
        return jnp.concatenate(
            [group_part, jnp.zeros((rows, ROUTER_EXPERT_LANE0 - N_GROUPS), F32), expert_part,
             jnp.zeros((rows, LANES - ROUTER_EXPERT_LANE0 - N_EXPERTS), F32)], axis=1)

    w_router = router_lanes(w_group_router[l], w_expert_router[l])
    b_router = router_lanes(b_group_router[l][None, :], b_expert_router[l][None, :])
    s5p = _s5_params(ssm_lambda_re[l], ssm_lambda_im[l], ssm_log_step[l], ssm_b_re[l], ssm_b_im[l],
                     ssm_c_re[l], ssm_c_im[l], ssm_d[l], S5_CHUNK // SUBLANES)
    bias = sb_logit_bias[l]

    mn = rmsnorm(mem_prompt.reshape(n_mem, d), g_mem_norm[l], BF16)
    (mem_kv,) = matmul(mn, w_memkv, (F32,), tm=n_mem, tn=1024, name="memory_kv")
    mk_p, mv_p = mem_kv[:, :mem_w], mem_kv[:, mem_w:]

    def mixer_and_moe(x, attend, mem_k, mem_v, s5):
        hn = rmsnorm(x, g_mix_norm[l], BF16)
        u, q, k, kb, v, vb, qm = in_projection(hn, w5, ssm_w)
        o_ssm, h_re, h_im = s5(u)
        o_sb = attend(q, k, v, kb, vb)
        o_mem = memory_attention(qm.reshape(mem_k.shape[0], -1, mem_w), mem_k, mem_v).reshape(-1, mem_w)
        merged = gated_merge(hn, (o_ssm, o_sb, o_mem), w_gates, w_branches)
        (x2,) = matmul(merged, w_out_b, (F32,), res=x, tm=512, tn=d, name="out_projection")
        moe_tile = min(MOE_TILE, x2.shape[0])
        hn2, route, counts = moe_router(x2, g_ffn_norm[l], w_router, b_router, moe_tile)
        y = moe_experts_final_norm(hn2, route, counts, x2, w_gate_up, w_down, g_final, moe_tile)
        return y, k, v, h_re, h_im

    zeros = jnp.zeros((1, n_st), F32)
    y_p, k_p, v_p, hr_p, hi_p = mixer_and_moe(
        x_prompt.reshape(tp, d),
        lambda q, k, v, kb, vb: sb_attention_prompt(q, kb, vb, bias),
        mk_p.reshape(1, n_mem, mem_w), mv_p.reshape(1, n_mem, mem_w),
        lambda u: s5_prompt(u, zeros, zeros, s5p, w_glu_b, b_glu[l], S5_CHUNK))

    def s5_s(u):
        u_tb = u.reshape(bs, ts, ssm_w).transpose(1, 0, 2).reshape(ts * bs, ssm_w)
        o_tb, h_re, h_im = s5_sample(u_tb, state_ssm_re[l].reshape(bs, n_st), state_ssm_im[l].reshape(bs, n_st),
                                     s5p, w_glu_b, b_glu[l], bs, ts)
        return o_tb.reshape(ts, bs, ssm_w).transpose(1, 0, 2).reshape(bs * ts, ssm_w), h_re, h_im

    def attend_s(q, k, v, kb, vb):
        o = sb_attention_sample(q.reshape(bs, ts, sb_w), k.reshape(bs, ts, sb_w), v.reshape(bs, ts, sb_w), bias,
                                cache_k.reshape(-1, PAGE_SIZE * n_heads, SB_HEAD_DIM),
                                cache_v.reshape(-1, PAGE_SIZE * n_heads, SB_HEAD_DIM),
                                page_table + l * cache_k.shape[1])
        return o.reshape(bs * ts, sb_w).astype(BF16)

    y_s, k_s, v_s, hr_s, hi_s = mixer_and_moe(
        x_sample.reshape(bs * ts, d), attend_s,
        cache_mem_k[l].reshape(bs, n_mem, mem_w), cache_mem_v[l].reshape(bs, n_mem, mem_w), s5_s)

    return (y_p.reshape(bp, tp, d), y_s.reshape(bs, ts, d),
            k_p.reshape(1, bp, tp, n_heads, SB_HEAD_DIM), v_p.reshape(1, bp, tp, n_heads, SB_HEAD_DIM),
            k_s.reshape(1, bs, ts, n_heads, SB_HEAD_DIM), v_s.reshape(1, bs, ts, n_heads, SB_HEAD_DIM),
            hr_p.reshape(1, bp, n_groups, n_state), hi_p.reshape(1, bp, n_groups, n_state),
            hr_s.reshape(1, bs, n_groups, n_state), hi_s.reshape(1, bs, n_groups, n_state),
            mk_p.reshape(1, bp, n_mem, MEM_HEADS, mem_dh), mv_p.reshape(1, bp, n_mem, MEM_HEADS, mem_dh))
```

```python
import functools
import math

import jax
import jax.numpy as jnp
from jax import lax
from jax.experimental import pallas as pl
from jax.experimental.pallas import tpu as pltpu

F32 = jnp.float32
BF16 = jnp.bfloat16

V7X_VMEM_BYTES = 64 * 1024 * 1024
VMEM_LIMIT = V7X_VMEM_BYTES - 8 * 1024 * 1024
LANES = 128
SUBLANES = 8

SSM_GROUP = 16
SSM_STATE = 64
SB_HEAD_DIM = 128
MEM_HEADS = 4
PAGE_SIZE = 128
N_GROUPS = 4
EXPERTS_PER_GROUP = 8
N_EXPERTS = N_GROUPS * EXPERTS_PER_GROUP
RMS_EPS = 1e-6

SSM_BLOCK_GROUPS = 16
SSM_BLOCK_CH = SSM_BLOCK_GROUPS * SSM_GROUP
SSM_BLOCK_ST = SSM_BLOCK_GROUPS * SSM_STATE
SCAN_LANES = 512
STRIP = 64
SAMPLE_SPLITS = 1
ROUTER_EXPERT_LANE0 = 32


def _cparams(*sem):
    return pltpu.CompilerParams(dimension_semantics=sem, vmem_limit_bytes=VMEM_LIMIT)


def _sigmoid(x):
    return 1.0 / (1.0 + jnp.exp(-x))


def _tile(m, pref):
    t = min(m, pref)
    assert m % t == 0, (m, pref)
    return t


def _rms(x, g):
    return x * lax.rsqrt(jnp.mean(x * x, axis=-1, keepdims=True) + RMS_EPS) * g


def _rms_kernel(x_ref, g_ref, o_ref):
    o_ref[...] = _rms(x_ref[...], g_ref[...]).astype(o_ref.dtype)


def rmsnorm(x, g, out_dtype, tm=512):
    m, d = x.shape
    tm = _tile(m, tm)
    return pl.pallas_call(
        _rms_kernel,
        grid=(m // tm,),
        in_specs=[pl.BlockSpec((tm, d), lambda i: (i, 0)), pl.BlockSpec((1, d), lambda i: (0, 0))],
        out_specs=pl.BlockSpec((tm, d), lambda i: (i, 0)),
        out_shape=jax.ShapeDtypeStruct((m, d), out_dtype),
        compiler_params=_cparams("parallel"),
        name="rmsnorm",
    )(x, g.reshape(1, d))


def _mm_kernel(*refs, has_res):
    a_ref, b_ref = refs[0], refs[1]
    o_refs = refs[3:] if has_res else refs[2:]
    acc = jnp.dot(a_ref[...], b_ref[...], preferred_element_type=F32)
    if has_res:
        acc = refs[2][...] + acc
    for o_ref in o_refs:
        o_ref[...] = acc.astype(o_ref.dtype)


def matmul(a, b, out_dtypes, res=None, tm=512, tn=1024, name="matmul"):
    m, k = a.shape
    n = b.shape[1]
    tm, tn = _tile(m, tm), _tile(n, tn)
    in_specs = [pl.BlockSpec((tm, k), lambda i, j: (i, 0)), pl.BlockSpec((k, tn), lambda i, j: (0, j))]
    args = [a, b]
    if res is not None:
        in_specs.append(pl.BlockSpec((tm, tn), lambda i, j: (i, j)))
        args.append(res)
    outs = pl.pallas_call(
        functools.partial(_mm_kernel, has_res=res is not None),
        grid=(m // tm, n // tn),
        in_specs=in_specs,
        out_specs=[pl.BlockSpec((tm, tn), lambda i, j: (i, j)) for _ in out_dtypes],
        out_shape=[jax.ShapeDtypeStruct((m, n), dt) for dt in out_dtypes],
        compiler_params=_cparams("parallel", "arbitrary"),
        name=name,
    )(*args)
    return outs


def _proj_kernel(a_ref, b_ref, u_ref, q_ref, k_ref, kb_ref, v_ref, vb_ref, qm_ref, *, q_scale):
    j = pl.program_id(1)
    acc = jnp.dot(a_ref[...], b_ref[...], preferred_element_type=F32)

    @pl.when(j == 0)
    def _():
        u_ref[...] = acc

    @pl.when(j == 1)
    def _():
        q_ref[...] = (acc * q_scale).astype(BF16)

    @pl.when(j == 2)
    def _():
        k_ref[...] = acc
        kb_ref[...] = acc.astype(BF16)

    @pl.when(j == 3)
    def _():
        v_ref[...] = acc
        vb_ref[...] = acc.astype(BF16)

    @pl.when(j == 4)
    def _():
        qm_ref[...] = acc.astype(BF16)


def in_projection(hn, w5, width, tm=512):
    m, k = hn.shape
    tm = _tile(m, tm)
    dts = (F32, BF16, F32, BF16, F32, BF16, BF16)
    return pl.pallas_call(
        functools.partial(_proj_kernel, q_scale=-(SB_HEAD_DIM ** -0.5)),
        grid=(m // tm, 5),
        in_specs=[pl.BlockSpec((tm, k), lambda i, j: (i, 0)), pl.BlockSpec((k, width), lambda i, j: (0, j))],
        out_specs=[pl.BlockSpec((tm, width), lambda i, j: (i, 0)) for _ in dts],
        out_shape=[jax.ShapeDtypeStruct((m, width), dt) for dt in dts],
        compiler_params=_cparams("parallel", "arbitrary"),
        name="in_projection",
    )(hn, w5)


def _cmul_add(ar, ai, hr, hi, xr, xi):
    return ar * hr - ai * hi + xr, ar * hi + ai * hr + xi


def _s5_readout(q, x_ref, u_ref, cblk_ref, d_ref, z_ref):
    cols = slice(q * SSM_BLOCK_CH, (q + 1) * SSM_BLOCK_CH)
    y = jnp.dot(x_ref[...].astype(BF16), cblk_ref[q], preferred_element_type=F32)
    y = y + d_ref[:, cols] * u_ref[:, cols]
    z_ref[:, cols] = jax.nn.gelu(y)


def _s5_glu(z_ref, wglu_ref, bglu_ref):
    z = z_ref[...]
    gate = _sigmoid(jnp.dot(z.astype(BF16), wglu_ref[...], preferred_element_type=F32) + bglu_ref[...])
    return z * gate


def _s5_prompt_kernel(u_ref, h0r_ref, h0i_ref, bblk_ref, cblk_ref, d_ref, apr_ref, api_ref, wglu_ref, bglu_ref,
                      o_ref, hr_out_ref, hi_out_ref, x_ref, z_ref, hcr_ref, hci_ref, *, nj):
    n_blocks = bblk_ref.shape[0]

    @pl.when(pl.program_id(0) == 0)
    def _():
        hcr_ref[...] = h0r_ref[...]
        hci_ref[...] = h0i_ref[...]

    for q in range(n_blocks):
        ub = u_ref[:, q * SSM_BLOCK_CH:(q + 1) * SSM_BLOCK_CH]
        x_ref[...] = jnp.dot(ub.astype(BF16), bblk_ref[q], preferred_element_type=F32)
        for c in range(SSM_BLOCK_ST // SCAN_LANES):
            re = slice(c * SCAN_LANES, (c + 1) * SCAN_LANES)
            im = slice(SSM_BLOCK_ST + c * SCAN_LANES, SSM_BLOCK_ST + (c + 1) * SCAN_LANES)
            st = slice(q * SSM_BLOCK_ST + c * SCAN_LANES, q * SSM_BLOCK_ST + (c + 1) * SCAN_LANES)
            ar = jnp.broadcast_to(apr_ref[0:1, st], (SUBLANES, SCAN_LANES))
            ai = jnp.broadcast_to(api_ref[0:1, st], (SUBLANES, SCAN_LANES))

            def scan_body(j, carry, re=re, im=im, ar=ar, ai=ai):
                rows = pl.ds(pl.multiple_of(j * SUBLANES, SUBLANES), SUBLANES)
                hr, hi = _cmul_add(ar, ai, carry[0], carry[1], x_ref[rows, re], x_ref[rows, im])
                x_ref[rows, re] = hr
                x_ref[rows, im] = hi
                return hr, hi

            zero = jnp.zeros((SUBLANES, SCAN_LANES), F32)
            er, ei = lax.fori_loop(0, nj, scan_body, (zero, zero), unroll=2)

            alr, ali = apr_ref[nj - 1:nj, st], api_ref[nj - 1:nj, st]
            cr, ci = hcr_ref[:, st], hci_ref[:, st]
            crs, cis = [], []
            for s in range(SUBLANES):
                crs.append(cr)
                cis.append(ci)
                cr, ci = _cmul_add(alr, ali, cr, ci, er[s:s + 1, :], ei[s:s + 1, :])
            hcr_ref[:, st] = cr
            hci_ref[:, st] = ci
            cin_r = jnp.concatenate(crs, axis=0)
            cin_i = jnp.concatenate(cis, axis=0)

            def fix_body(j, _, re=re, im=im, st=st, cin_r=cin_r, cin_i=cin_i):
                rows = pl.ds(pl.multiple_of(j * SUBLANES, SUBLANES), SUBLANES)
                pr = jnp.broadcast_to(apr_ref[pl.ds(j, 1), st], (SUBLANES, SCAN_LANES))
                pi = jnp.broadcast_to(api_ref[pl.ds(j, 1), st], (SUBLANES, SCAN_LANES))
                hr, hi = _cmul_add(pr, pi, cin_r, cin_i, x_ref[rows, re], x_ref[rows, im])
                x_ref[rows, re] = hr
                x_ref[rows, im] = hi
                return 0

            lax.fori_loop(0, nj, fix_body, 0, unroll=2)
        _s5_readout(q, x_ref, u_ref, cblk_ref, d_ref, z_ref)

    o_ref[...] = _s5_glu(z_ref, wglu_ref, bglu_ref).astype(o_ref.dtype)
    hr_out_ref[...] = hcr_ref[...]
    hi_out_ref[...] = hci_ref[...]


def _s5_sample_kernel(u_ref, h0r_ref, h0i_ref, bblk_ref, cblk_ref, d_ref, apr_ref, api_ref, wglu_ref, bglu_ref,
                      o_ref, hr_out_ref, hi_out_ref, x_ref, z_ref, *, nb, nt):
    n_blocks = bblk_ref.shape[0]
    for q in range(n_blocks):
        ub = u_ref[:, q * SSM_BLOCK_CH:(q + 1) * SSM_BLOCK_CH]
        x_ref[...] = jnp.dot(ub.astype(BF16), bblk_ref[q], preferred_element_type=F32)
        for c in range(SSM_BLOCK_ST // SCAN_LANES):
            re = slice(c * SCAN_LANES, (c + 1) * SCAN_LANES)
            im = slice(SSM_BLOCK_ST + c * SCAN_LANES, SSM_BLOCK_ST + (c + 1) * SCAN_LANES)
            st = slice(q * SSM_BLOCK_ST + c * SCAN_LANES, q * SSM_BLOCK_ST + (c + 1) * SCAN_LANES)
            ar, ai = apr_ref[0:1, st], api_ref[0:1, st]
            hr, hi = h0r_ref[:, st], h0i_ref[:, st]
            for t in range(nt):
                rows = slice(t * nb, (t + 1) * nb)
                hr, hi = _cmul_add(ar, ai, hr, hi, x_ref[rows, re], x_ref[rows, im])
                x_ref[rows, re] = hr
                x_ref[rows, im] = hi
            hr_out_ref[:, st] = hr
            hi_out_ref[:, st] = hi
        _s5_readout(q, x_ref, u_ref, cblk_ref, d_ref, z_ref)
    o_ref[...] = _s5_glu(z_ref, wglu_ref, bglu_ref).astype(o_ref.dtype)


def _s5_params(lam_re, lam_im, log_step, b_re, b_im, c_re, c_im, d_skip, n_pow):
    g, p = lam_re.shape
    dt = jnp.exp(log_step)[:, None]
    steps = jnp.arange(1, n_pow + 1, dtype=F32)[:, None, None]
    mag = jnp.exp(lam_re * dt * steps)
    phase = lam_im * dt * steps
    ap_re = (mag * jnp.cos(phase)).reshape(n_pow, g * p)
    ap_im = (mag * jnp.sin(phase)).reshape(n_pow, g * p)
    a_re, a_im = mag[0] * jnp.cos(phase[0]), mag[0] * jnp.sin(phase[0])
    num_re = a_re - 1.0
    den = lam_re * lam_re + lam_im * lam_im
    k_re = (num_re * lam_re + a_im * lam_im) / den
    k_im = (a_im * lam_re - num_re * lam_im) / den
    bp_re = k_re[..., None] * b_re - k_im[..., None] * b_im
    bp_im = k_re[..., None] * b_im + k_im[..., None] * b_re
    nblk = g // SSM_BLOCK_GROUPS
    eye = jnp.eye(SSM_BLOCK_GROUPS, dtype=F32)

    def b_block(bp):
        bp = bp.reshape(nblk, SSM_BLOCK_GROUPS, p, SSM_GROUP)
        return jnp.einsum('qgpc,gh->qgchp', bp, eye).reshape(nblk, SSM_BLOCK_CH, SSM_BLOCK_ST)

    def c_block(c):
        c = c.reshape(nblk, SSM_BLOCK_GROUPS, SSM_GROUP, p)
        return jnp.einsum('qgcp,gh->qgphc', c, eye).reshape(nblk, SSM_BLOCK_ST, SSM_BLOCK_CH)

    bblk = jnp.concatenate([b_block(bp_re), b_block(bp_im)], axis=2).astype(BF16)
    cblk = jnp.concatenate([c_block(c_re), -c_block(c_im)], axis=1).astype(BF16)
    return bblk, cblk, d_skip.reshape(1, g * SSM_GROUP), ap_re, ap_im


def _const_spec(shape):
    return pl.BlockSpec(shape, lambda *_: (0,) * len(shape))


def s5_prompt(u, h0r, h0i, params, w_glu, b_glu, chunk):
    t, w = u.shape
    nj = chunk // SUBLANES
    bblk, cblk, d, apr, api = params
    ns = apr.shape[1]
    consts = [h0r, h0i, bblk, cblk, d, apr, api, w_glu, b_glu.reshape(1, w)]
    u_perm = u.reshape(t // chunk, SUBLANES, nj, w).transpose(0, 2, 1, 3).reshape(t, w)
    o_perm, h_re, h_im = pl.pallas_call(
        functools.partial(_s5_prompt_kernel, nj=nj),
        grid=(t // chunk,),
        in_specs=[pl.BlockSpec((chunk, w), lambda i: (i, 0))] + [_const_spec(c.shape) for c in consts],
        out_specs=[pl.BlockSpec((chunk, w), lambda i: (i, 0)), _const_spec((1, ns)), _const_spec((1, ns))],
        out_shape=[jax.ShapeDtypeStruct((t, w), BF16), jax.ShapeDtypeStruct((1, ns), F32),
                   jax.ShapeDtypeStruct((1, ns), F32)],
        scratch_shapes=[pltpu.VMEM((chunk, 2 * SSM_BLOCK_ST), F32), pltpu.VMEM((chunk, w), F32),
                        pltpu.VMEM((1, ns), F32), pltpu.VMEM((1, ns), F32)],
        compiler_params=_cparams("arbitrary"),
        name="s5_prompt",
    )(u_perm, *consts)
    o = o_perm.reshape(t // chunk, nj, SUBLANES, w).transpose(0, 2, 1, 3).reshape(t, w)
    return o, h_re, h_im


def s5_sample(u_tb, h0r, h0i, params, w_glu, b_glu, nb, nt):
    m, w = u_tb.shape
    bblk, cblk, d, apr, api = params
    ns = apr.shape[1]
    consts = [h0r, h0i, bblk, cblk, d, apr, api, w_glu, b_glu.reshape(1, w)]
    return pl.pallas_call(
        functools.partial(_s5_sample_kernel, nb=nb, nt=nt),
        grid=(1,),
        in_specs=[_const_spec(u_tb.shape)] + [_const_spec(c.shape) for c in consts],
        out_specs=[_const_spec((m, w)), _const_spec((nb, ns)), _const_spec((nb, ns))],
        out_shape=[jax.ShapeDtypeStruct((m, w), BF16), jax.ShapeDtypeStruct((nb, ns), F32),
                   jax.ShapeDtypeStruct((nb, ns), F32)],
        scratch_shapes=[pltpu.VMEM((m, 2 * SSM_BLOCK_ST), F32), pltpu.VMEM((m, w), F32)],
        compiler_params=_cparams("arbitrary"),
        name="s5_sample",
    )(u_tb, *consts)


def _log_sigmoid_pair(w):
    sign_bit = jnp.uint32(0x80000000)
    neg_abs = lax.bitcast_convert_type(lax.bitcast_convert_type(w, jnp.uint32) | sign_bit, F32)
    lk = jnp.minimum(w, 0.0) - jnp.log(1.0 + jnp.exp(neg_abs))
    return lk - w, lk


def _split3_bf16(x):
    hi = x.astype(BF16).astype(F32)
    mid = (x - hi).astype(BF16).astype(F32)
    lo = (x - hi - mid).astype(BF16).astype(F32)
    return hi, mid, lo


def _sb_prompt_kernel(bias_ref, q_ref, k_ref, v_ref, o_ref, acc_ref, carry_ref, w_ref, *, tq, tk, n_hp):
    hp, i = pl.program_id(0), pl.program_id(1)
    kb_per_q = tq // tk
    row = lax.broadcasted_iota(jnp.int32, (tq, tk), 0)
    col = lax.broadcasted_iota(jnp.int32, (tq, tk), 1)
    newer = (lax.broadcasted_iota(jnp.int32, (tk, tk), 0)
             > lax.broadcasted_iota(jnp.int32, (tk, tk), 1)).astype(BF16)
    acc_ref[...] = jnp.zeros_like(acc_ref)
    carry_ref[...] = jnp.zeros_like(carry_ref)

    lane_q = lax.broadcasted_iota(jnp.int32, (tq, SB_HEAD_DIM), 1)
    lane_k = lax.broadcasted_iota(jnp.int32, (tk, SB_HEAD_DIM), 1)
    q_units = jnp.where(lane_q < 3, 1.0, 0.0).astype(BF16)
    q_aug, k_bias = [], []
    for hh in range(n_hp):
        hi, mid, lo = _split3_bf16(jnp.full((tk, SB_HEAD_DIM), -bias_ref[hp * n_hp + hh], F32))
        k_bias.append(jnp.where(lane_k == 0, hi, jnp.where(lane_k == 1, mid, jnp.where(lane_k == 2, lo, 0.0)))
                      .astype(BF16))
        q_aug.append(jnp.concatenate([q_ref[:, hh * SB_HEAD_DIM:(hh + 1) * SB_HEAD_DIM], q_units], axis=1))

    def logits(kb, hh):
        rows = pl.ds(pl.multiple_of(kb * tk, tk), tk)
        k_aug = jnp.concatenate([k_ref[rows, hh * SB_HEAD_DIM:(hh + 1) * SB_HEAD_DIM], k_bias[hh]], axis=1)
        return lax.dot_general(q_aug[hh], k_aug, (((1,), (1,)), ((), ())), preferred_element_type=F32)

    def weigh(kb, hh, w, mask):
        rows = pl.ds(pl.multiple_of(kb * tk, tk), tk)
        ls_s, lk_s, lkb_s = [], [], []
        for s in range(tq // STRIP):
            r = slice(s * STRIP, (s + 1) * STRIP)
            ls, lk = _log_sigmoid_pair(w[r])
            if mask is not None:
                lk = jnp.where(mask[r], lk, 0.0)
            ls_s.append(ls)
            lk_s.append(jnp.sum(lk, axis=1, keepdims=True))
            lkb_s.append(lk.astype(BF16))
        cum = jnp.dot(jnp.concatenate(lkb_s, axis=0), newer, preferred_element_type=F32)
        a_s = []
        for s in range(tq // STRIP):
            r = slice(s * STRIP, (s + 1) * STRIP)
            a = jnp.exp(ls_s[s] + cum[r] + carry_ref[hh, r, :])
            if mask is not None:
                a = jnp.where(mask[r], a, 0.0)
            a_s.append(a.astype(BF16))
        acc_ref[hh] += jnp.dot(jnp.concatenate(a_s, axis=0), v_ref[rows, hh * SB_HEAD_DIM:(hh + 1) * SB_HEAD_DIM],
                               preferred_element_type=F32)
        carry_ref[hh] += jnp.concatenate(lk_s, axis=0)

    def tile(kb, mask):
        for hh in range(n_hp):
            weigh(kb, hh, logits(kb, hh), mask)

    for d in reversed(range(kb_per_q)):
        tile(i * kb_per_q + d, col + d * tk < row)

    for hh in range(n_hp):
        w_ref[hh] = logits(jnp.maximum(i * kb_per_q - 1, 0), hh)

    def body(jj, _):
        base = (i - jj) * kb_per_q
        for hh in range(n_hp):
            weigh(base - 1, hh, w_ref[hh], None)
        for d in range(1, kb_per_q):
            tile(base - 1 - d, None)
        for hh in range(n_hp):
            w_ref[hh] = logits(jnp.maximum(base - kb_per_q - 1, 0), hh)
        return 0

    lax.fori_loop(0, i, body, 0)
    for hh in range(n_hp):
        o_ref[:, hh * SB_HEAD_DIM:(hh + 1) * SB_HEAD_DIM] = acc_ref[hh].astype(o_ref.dtype)


def sb_attention_prompt(q, k, v, bias, tq=512, tk=256, heads_per_step=4):
    t, w = q.shape
    n_heads = w // SB_HEAD_DIM
    tq, tk, n_hp = _tile(t, tq), _tile(t, tk), heads_per_step
    assert tq % tk == 0 and n_heads % n_hp == 0
    wb = n_hp * SB_HEAD_DIM
    grid_spec = pltpu.PrefetchScalarGridSpec(
        num_scalar_prefetch=1,
        grid=(n_heads // n_hp, t // tq),
        in_specs=[pl.BlockSpec((tq, wb), lambda h, i, b: (i, h)),
                  pl.BlockSpec((t, wb), lambda h, i, b: (0, h), pipeline_mode=pl.Buffered(1)),
                  pl.BlockSpec((t, wb), lambda h, i, b: (0, h), pipeline_mode=pl.Buffered(1))],
        out_specs=pl.BlockSpec((tq, wb), lambda h, i, b: (i, h)),
        scratch_shapes=[pltpu.VMEM((n_hp, tq, SB_HEAD_DIM), F32), pltpu.VMEM((n_hp, tq, 1), F32),
                        pltpu.VMEM((n_hp, tq, tk), F32)],
    )
    return pl.pallas_call(
        functools.partial(_sb_prompt_kernel, tq=tq, tk=tk, n_hp=n_hp),
        grid_spec=grid_spec,
        out_shape=jax.ShapeDtypeStruct((t, w), BF16),
        compiler_params=_cparams("parallel", "arbitrary"),
        name="sb_attention_prompt",
    )(bias, q, k, v)


def _sb_sample_kernel(pt_ref, *refs, n_pp, n_heads):
    wq_ref, bias_ref, kn_ref, vn_ref = refs[:4]
    kp_refs = refs[4:4 + n_pp]
    vp_refs = refs[4 + n_pp:4 + 2 * n_pp]
    o_ref, acc_ref, carry_ref = refs[4 + 2 * n_pp:]
    p = pl.program_id(1)
    tk = PAGE_SIZE
    nq = o_ref.shape[0]
    dh = SB_HEAD_DIM
    newer = (lax.broadcasted_iota(jnp.int32, (tk, tk), 1)
             > lax.broadcasted_iota(jnp.int32, (tk, tk), 0)).astype(BF16)

    def head_rows(ref, h):
        return ref[pl.ds(h, tk, stride=n_heads), :]

    def weights(z, mask, carry):
        ls, lk = _log_sigmoid_pair(z)
        if mask is not None:
            lk = jnp.where(mask, lk, 0.0)
        after = jnp.dot(newer, lk.astype(BF16), preferred_element_type=F32) + carry
        a = jnp.exp(ls + after)
        if mask is not None:
            a = jnp.where(mask, a, 0.0)
        return a.T, jnp.sum(lk, axis=0, keepdims=True)

    @pl.when(p == 0)
    def _():
        lane_q = lax.broadcasted_iota(jnp.int32, (tk, LANES), 1) % nq
        key = lax.broadcasted_iota(jnp.int32, (tk, LANES), 0)
        fill = jnp.zeros((tk - nq, kn_ref.shape[1]), F32)
        kn = jnp.concatenate([kn_ref[...], fill], axis=0).astype(BF16)
        vn = jnp.concatenate([vn_ref[...], fill], axis=0).astype(BF16)
        z = jnp.dot(kn, wq_ref[...], preferred_element_type=F32) - bias_ref[...]
        at, tot = weights(z, key < lane_q, jnp.zeros((1, LANES), F32))
        carry_ref[...] = tot
        for h in range(n_heads):
            acc_ref[h * nq:(h + 1) * nq, :] = jnp.dot(at[h * nq:(h + 1) * nq, :].astype(BF16),
                                                      vn[:, h * dh:(h + 1) * dh], preferred_element_type=F32)

    per = n_pp // SAMPLE_SPLITS
    carry = carry_ref[...]
    partial = [None] * n_heads
    for sp in range(SAMPLE_SPLITS):
        pages = list(range(sp * per, (sp + 1) * per))
        z = -bias_ref[...]
        for hp in range(n_heads // 2):
            lhs = jnp.concatenate(
                [jnp.concatenate([head_rows(kp_refs[r], 2 * hp), head_rows(kp_refs[r], 2 * hp + 1)], axis=1)
                 for r in pages], axis=0).astype(BF16)
            z = z + jnp.dot(lhs, wq_ref[2 * hp * dh:(2 * hp + 2) * dh, :], preferred_element_type=F32)
        ats = []
        for idx in range(per):
            at, tot = weights(z[idx * tk:(idx + 1) * tk, :], None, carry)
            carry = carry + tot
            ats.append(at)
        for h in range(n_heads):
            lhs = jnp.concatenate([at[h * nq:(h + 1) * nq, :] for at in ats], axis=1).astype(BF16)
            rhs = jnp.concatenate([head_rows(vp_refs[r], h) for r in pages], axis=0).astype(BF16)
            term = jnp.dot(lhs, rhs, preferred_element_type=F32)
            partial[h] = term if partial[h] is None else partial[h] + term
    carry_ref[...] = carry
    for h in range(n_heads):
        acc_ref[h * nq:(h + 1) * nq, :] += partial[h]

    @pl.when(p == pl.num_programs(1) - 1)
    def _():
        for h in range(n_heads):
            o_ref[:, h * dh:(h + 1) * dh] = acc_ref[h * nq:(h + 1) * nq, :]


def sb_attention_sample(q, k_new, v_new, bias, cache_k, cache_v, page_table, pages_per_step=8):
    nb, nq, w = q.shape
    n_heads = w // SB_HEAD_DIM
    n_pages = page_table.shape[1]
    n_pp = pages_per_step
    assert n_pages % n_pp == 0 and n_heads * nq <= LANES and nq <= PAGE_SIZE and n_heads % 2 == 0
    assert LANES % nq == 0 and nq % SUBLANES == 0
    q4 = q.reshape(nb, nq, n_heads, SB_HEAD_DIM).transpose(0, 2, 3, 1)
    same_head = jnp.arange(n_heads)[:, None] == jnp.arange(LANES // nq)[None, :]
    wq = jnp.where(same_head[None, :, None, :, None], q4[:, :, :, None, :], 0).reshape(nb, w, LANES)
    bias_row = jnp.pad(jnp.repeat(bias, nq), (0, LANES - n_heads * nq)).reshape(1, LANES)

    page_rows = PAGE_SIZE * n_heads

    def page_spec(r):
        return pl.BlockSpec((None, page_rows, SB_HEAD_DIM),
                            lambda b, p, pt: (pt[b, n_pages - 1 - (p * n_pp + r)], 0, 0))

    per_b = lambda b, p, pt: (b, 0, 0)
    grid_spec = pltpu.PrefetchScalarGridSpec(
        num_scalar_prefetch=1,
        grid=(nb, n_pages // n_pp),
        in_specs=[pl.BlockSpec((None, w, LANES), per_b), pl.BlockSpec((1, LANES), lambda b, p, pt: (0, 0)),
                  pl.BlockSpec((None, nq, w), per_b), pl.BlockSpec((None, nq, w), per_b)]
                 + [page_spec(r) for r in range(n_pp)] + [page_spec(r) for r in range(n_pp)],
        out_specs=pl.BlockSpec((None, nq, w), per_b),
        scratch_shapes=[pltpu.VMEM((n_heads * nq, SB_HEAD_DIM), F32), pltpu.VMEM((1, LANES), F32)],
    )
    return pl.pallas_call(
        functools.partial(_sb_sample_kernel, n_pp=n_pp, n_heads=n_heads),
        grid_spec=grid_spec,
        out_shape=jax.ShapeDtypeStruct((nb, nq, w), F32),
        compiler_params=_cparams("parallel", "arbitrary"),
        name="sb_attention_sample",
    )(page_table, wq, bias_row, k_new, v_new, *([cache_k] * n_pp), *([cache_v] * n_pp))


def _mem_attn_kernel(q_ref, k_ref, v_ref, o_ref, *, scale):
    dh = o_ref.shape[1] // MEM_HEADS
    for h in range(MEM_HEADS):
        cols = slice(h * dh, (h + 1) * dh)
        kh = k_ref[:, cols].astype(BF16)
        vh = v_ref[:, cols].astype(BF16)
        s = lax.dot_general(q_ref[:, cols], kh, (((1,), (1,)), ((), ())), preferred_element_type=F32) * scale
        e = jnp.exp(s - jnp.max(s, axis=-1, keepdims=True))
        p = e / jnp.sum(e, axis=-1, keepdims=True)
        o_ref[:, cols] = jnp.dot(p.astype(BF16), vh, preferred_element_type=F32).astype(o_ref.dtype)


def memory_attention(q, mem_k, mem_v, tq=512):
    nb, t, w = q.shape
    n_mem = mem_k.shape[1]
    tq = _tile(t, tq)
    return pl.pallas_call(
        functools.partial(_mem_attn_kernel, scale=(w // MEM_HEADS) ** -0.5),
        grid=(nb, t // tq),
        in_specs=[pl.BlockSpec((None, tq, w), lambda b, i: (b, i, 0)),
                  pl.BlockSpec((None, n_mem, w), lambda b, i: (b, 0, 0)),
                  pl.BlockSpec((None, n_mem, w), lambda b, i: (b, 0, 0))],
        out_specs=pl.BlockSpec((None, tq, w), lambda b, i: (b, i, 0)),
        out_shape=jax.ShapeDtypeStruct((nb, t, w), BF16),
        compiler_params=_cparams("parallel", "arbitrary"),
        name="memory_attention",
    )(q, mem_k, mem_v)


def _merge_kernel(hn_ref, o0_ref, o1_ref, o2_ref, g0_ref, g1_ref, g2_ref, b0_ref, b1_ref, b2_ref, out_ref):
    hn = hn_ref[...]
    total = None
    for o_ref, g_ref, b_ref in ((o0_ref, g0_ref, b0_ref), (o1_ref, g1_ref, b1_ref), (o2_ref, g2_ref, b2_ref)):
        gate = _sigmoid(jnp.dot(hn, g_ref[...], preferred_element_type=F32))
        term = gate * jnp.dot(o_ref[...], b_ref[...], preferred_element_type=F32)
        total = term if total is None else total + term
    out_ref[...] = total.astype(out_ref.dtype)


def gated_merge(hn, branches, w_gates, w_branches, tm=512, tn=512):
    m, d = hn.shape
    tm, tn = _tile(m, tm), _tile(d, tn)
    nj = d // tn
    row = lambda i, j: (i, 0)
    in_specs = [pl.BlockSpec((tm, d), row)]
    in_specs += [pl.BlockSpec((tm, o.shape[1]), row) for o in branches]
    in_specs += [pl.BlockSpec((d, tn), functools.partial(lambda i, j, b: (0, b * nj + j), b=b)) for b in range(3)]
    in_specs += [pl.BlockSpec((w.shape[0], tn), lambda i, j: (0, j)) for w in w_branches]
    return pl.pallas_call(
        _merge_kernel,
        grid=(m // tm, nj),
        in_specs=in_specs,
        out_specs=pl.BlockSpec((tm, tn), lambda i, j: (i, j)),
        out_shape=jax.ShapeDtypeStruct((m, d), BF16),
        compiler_params=_cparams("parallel", "arbitrary"),
        name="gated_merge",
    )(hn, *branches, w_gates, w_gates, w_gates, *w_branches)


MOE_TILE = 1024
MOE_ROWS = 320
MOE_EXPERTS_PER_STEP = 2
ROUTER_GROUP_LANE = 64
ROUTER_RANK_LANE = 65


def _router_kernel(x_ref, g_ref, wr_ref, br_ref, hn_ref, gate_ref, cnt_ref, run_ref):
    hn = _rms(x_ref[...], g_ref[...])
    hn_ref[...] = hn.astype(hn_ref.dtype)
    logits = jnp.dot(hn, wr_ref[...], preferred_element_type=F32, precision=lax.Precision.HIGHEST) + br_ref[...]
    lane = lax.broadcasted_iota(jnp.int32, logits.shape, 1).astype(F32)
    neg, far = -1e30, 1e9
    rmax = lambda v: jnp.max(v, axis=-1, keepdims=True)
    rmin = lambda v: jnp.min(v, axis=-1, keepdims=True)
    rsum = lambda v: jnp.sum(v, axis=-1, keepdims=True)
    is_group = lane < N_GROUPS
    gl = jnp.where(is_group, logits, neg)
    gmax = rmax(gl)
    g_val = 1.0 / rsum(jnp.where(is_group, jnp.exp(gl - gmax), 0.0))
    g_idx = rmin(jnp.where(gl == gmax, lane, far))
    lo = ROUTER_EXPERT_LANE0 + EXPERTS_PER_GROUP * g_idx
    in_group = jnp.logical_and(lane >= lo, lane < lo + EXPERTS_PER_GROUP)
    el = jnp.where(in_group, logits, neg)
    ex = jnp.where(in_group, jnp.exp(el - rmax(el)), 0.0)
    prob = jnp.where(in_group, ex / rsum(ex), -1.0)
    v1 = rmax(prob)
    i1 = rmin(jnp.where(prob == v1, lane, far))
    rest = jnp.where(lane == i1, -1.0, prob)
    v2 = rmax(rest)
    i2 = rmin(jnp.where(rest == v2, lane, far))
    within = jnp.where(lane == i1, v1 / (v1 + v2), jnp.where(lane == i2, v2 / (v1 + v2), 0.0))

    @pl.when(pl.program_id(1) == 0)
    def _():
        run_ref[...] = jnp.zeros_like(run_ref)

    tm = logits.shape[0]
    chose = jnp.where(lane == g_idx, 1.0, 0.0)
    earlier = (lax.broadcasted_iota(jnp.int32, (tm, tm), 1)
               < lax.broadcasted_iota(jnp.int32, (tm, tm), 0)).astype(BF16)
    before = jnp.dot(earlier, chose.astype(BF16), preferred_element_type=F32) + run_ref[...]
    rank = rsum(chose * before)
    run_ref[...] += jnp.sum(chose, axis=0, keepdims=True)
    cnt_ref[...] = run_ref[...]
    gate_ref[...] = (g_val * within + jnp.where(lane == ROUTER_GROUP_LANE, g_idx, 0.0)
                     + jnp.where(lane == ROUTER_RANK_LANE, rank, 0.0))


def moe_router(x, g, w_router, b_router, tile, tm=256):
    m, d = x.shape
    tm = _tile(tile, tm)
    sub = tile // tm
    row = lambda i, s: (i * sub + s, 0)
    return pl.pallas_call(
        _router_kernel,
        grid=(m // tile, sub),
        in_specs=[pl.BlockSpec((tm, d), row), _const_spec((1, d)), _const_spec((d, LANES)), _const_spec((1, LANES))],
        out_specs=[pl.BlockSpec((tm, d), row), pl.BlockSpec((tm, LANES), row),
                   pl.BlockSpec((None, 1, LANES), lambda i, s: (i, 0, 0))],
        out_shape=[jax.ShapeDtypeStruct((m, d), BF16), jax.ShapeDtypeStruct((m, LANES), F32),
                   jax.ShapeDtypeStruct((m // tile, 1, LANES), F32)],
        scratch_shapes=[pltpu.VMEM((1, LANES), F32)],
        compiler_params=_cparams("parallel", "arbitrary"),
        name="moe_router",
    )(x, g.reshape(1, d), w_router, b_router)


def _experts_kernel(bg_ref, nb_ref, hn_ref, gate_ref, x_ref, drow_ref, dcol_ref, wgu_ref, wd_ref, gf_ref, o_ref,
                    xs_ref, gs_ref, yb_ref):
    i, b, j = pl.program_id(0), pl.program_id(1), pl.program_id(2)
    tile = hn_ref.shape[0]
    rows = xs_ref.shape[0]
    ff = wd_ref.shape[1]
    active = b < nb_ref[i]
    last_j = j == pl.num_programs(2) - 1

    @pl.when(jnp.logical_and(b == 0, j == 0))
    def _():
        o_ref[...] = x_ref[...]

    @pl.when(jnp.logical_and(active, j == 0))
    def _():
        slot = lax.broadcasted_iota(jnp.int32, (rows, tile), 0) + b * rows
        pick = jnp.where(drow_ref[...] == slot, 1.0, 0.0).astype(BF16)
        xs_ref[...] = jnp.dot(pick, hn_ref[...], preferred_element_type=F32).astype(BF16)
        gs = None
        for part in _split3_bf16(gate_ref[...]):
            term = jnp.dot(pick, part.astype(BF16), preferred_element_type=F32)
            gs = term if gs is None else gs + term
        gs_ref[...] = gs
        yb_ref[...] = jnp.zeros_like(yb_ref)

    @pl.when(active)
    def _():
        lane = lax.broadcasted_iota(jnp.int32, gs_ref.shape, 1)
        y = yb_ref[...]
        for k in range(MOE_EXPERTS_PER_STEP):
            e_lane = ROUTER_EXPERT_LANE0 + bg_ref[i, b] * EXPERTS_PER_GROUP + j * MOE_EXPERTS_PER_STEP + k
            gate = jnp.sum(jnp.where(lane == e_lane, gs_ref[...], 0.0), axis=-1, keepdims=True)
            au = jnp.dot(xs_ref[...], wgu_ref[k], preferred_element_type=F32)
            a, up = au[:, :ff], au[:, ff:]
            act = a * _sigmoid(a) * up * gate
            y = y + jnp.dot(act.astype(BF16), wd_ref[k], preferred_element_type=F32)
        yb_ref[...] = y

    @pl.when(jnp.logical_and(active, last_j))
    def _():
        slot = lax.broadcasted_iota(jnp.int32, (tile, rows), 1) + b * rows
        put = jnp.where(dcol_ref[...] == slot, 1.0, 0.0).astype(BF16)
        o_ref[...] += jnp.dot(put, yb_ref[...].astype(BF16), preferred_element_type=F32)

    @pl.when(jnp.logical_and(b == pl.num_programs(1) - 1, last_j))
    def _():
        o_ref[...] = _rms(o_ref[...], gf_ref[...])


def moe_experts_final_norm(hn, route, counts, x, w_gate_up, w_down, g_final, tile):
    m, d = x.shape
    n_exp, _, ff2 = w_gate_up.shape
    n_tiles = m // tile
    rows = min(MOE_ROWS, tile)
    n_blocks = (tile + N_GROUPS * (rows - 1)) // rows
    per = MOE_EXPERTS_PER_STEP
    steps = EXPERTS_PER_GROUP // per
    w_gate_up = w_gate_up.reshape(n_exp // per, per, d, ff2)
    w_down = w_down.reshape(n_exp // per, per, ff2 // 2, d)
    cnt = counts[:, 0, :N_GROUPS].astype(jnp.int32)
    blocks_g = (cnt + rows - 1) // rows
    end_g = jnp.cumsum(blocks_g, axis=1)
    start_g = end_g - blocks_g
    n_blk = end_g[:, -1]
    bidx = jnp.arange(n_blocks, dtype=jnp.int32)[None, :]
    blk_group = jnp.sum(bidx[:, :, None] >= end_g[:, None, :], axis=2).astype(jnp.int32)
    last_group = jnp.sum(jnp.where(bidx == jnp.maximum(n_blk - 1, 0)[:, None], blk_group, 0), axis=1, keepdims=True)
    blk_group = jnp.minimum(blk_group, last_group)
    grp = route[:, ROUTER_GROUP_LANE].astype(jnp.int32).reshape(n_tiles, tile)
    rank = route[:, ROUTER_RANK_LANE].astype(jnp.int32).reshape(n_tiles, tile)
    start_t = sum(jnp.where(grp == g, start_g[:, g:g + 1], 0) for g in range(N_GROUPS))
    dest = rows * start_t + rank

    def w_index(i, b, j, bg, nb):
        live = b < nb[i]
        last = jnp.maximum(nb[i] - 1, 0)
        return (jnp.where(live, bg[i, b] * steps + j, bg[i, last] * steps + steps - 1), 0, 0, 0)

    row = lambda i, b, j, bg, nb: (i, 0)
    grid_spec = pltpu.PrefetchScalarGridSpec(
        num_scalar_prefetch=2,
        grid=(n_tiles, n_blocks, steps),
        in_specs=[pl.BlockSpec((tile, d), row, pipeline_mode=pl.Buffered(1)), pl.BlockSpec((tile, LANES), row),
                  pl.BlockSpec((tile, d), row, pipeline_mode=pl.Buffered(1)),
                  pl.BlockSpec((None, 1, tile), lambda i, b, j, bg, nb: (i, 0, 0)),
                  pl.BlockSpec((tile, 1), row),
                  pl.BlockSpec((None, per, d, ff2), w_index), pl.BlockSpec((None, per, ff2 // 2, d), w_index),
                  pl.BlockSpec((1, d), lambda i, b, j, bg, nb: (0, 0))],
        out_specs=pl.BlockSpec((tile, d), row),
        scratch_shapes=[pltpu.VMEM((rows, d), BF16), pltpu.VMEM((rows, LANES), F32), pltpu.VMEM((rows, d), F32)],
    )
    return pl.pallas_call(
        _experts_kernel,
        grid_spec=grid_spec,
        out_shape=jax.ShapeDtypeStruct((m, d), F32),
        compiler_params=_cparams("parallel", "arbitrary", "arbitrary"),
        name="moe_experts",
    )(blk_group, n_blk, hn, route, x, dest.reshape(n_tiles, 1, tile), dest.reshape(m, 1), w_gate_up, w_down,
      g_final.reshape(1, d))


S5_CHUNK = 512


def kernel(x_prompt, x_sample, mem_prompt, cache_k, cache_v, page_table, cache_mem_k, cache_mem_v, state_ssm_re, state_ssm_im, g_mix_norm, w_in, sb_logit_bias, ssm_lambda_re, ssm_lambda_im, ssm_log_step, ssm_b_re, ssm_b_im, ssm_c_re, ssm_c_im, ssm_d, w_glu, b_glu, g_mem_norm, w_mem_k, w_mem_v, w_branch_ssm, w_branch_sb, w_branch_mem, w_out, g_ffn_norm, w_group_router, b_group_router, w_expert_router, b_expert_router, w_exp_gate, w_exp_up, w_exp_down, g_final):
    depth = w_in.shape[0]
    assert depth == 1, "one mixer + MoE layer per step"
    bp, tp, d = x_prompt.shape
    bs, ts, _ = x_sample.shape
    assert bp == 1
    n_groups, n_state = ssm_lambda_re.shape[1:]
    ssm_w = n_groups * SSM_GROUP
    n_st = n_groups * n_state
    sb_w = cache_k.shape[3] * cache_k.shape[4]
    mem_w = cache_mem_k.shape[3] * cache_mem_k.shape[4]
    n_mem = mem_prompt.shape[1]
    assert ssm_w == sb_w == mem_w
    n_heads = sb_w // SB_HEAD_DIM
    mem_dh = mem_w // MEM_HEADS
    l = 0

    w5 = w_in[l, :, :5 * ssm_w].astype(BF16)
    w_gates = w_in[l, :, 5 * ssm_w:].astype(BF16)
    w_branches = [w_branch_ssm[l].astype(BF16), w_branch_sb[l].astype(BF16), w_branch_mem[l].astype(BF16)]
    w_out_b = w_out[l].astype(BF16)
    w_glu_b = w_glu[l].astype(BF16)
    w_memkv = jnp.concatenate([w_mem_k[l], w_mem_v[l]], axis=1).astype(BF16)
    w_gate_up = jnp.concatenate([w_exp_gate[l], w_exp_up[l]], axis=2).astype(BF16)
    w_down = w_exp_down[l].astype(BF16)
    def router_lanes(group_part, expert_part):
        rows = group_part.shape[0]
        return jnp.concatenate(
            [group_part, jnp.zeros((rows, ROUTER_EXPERT_LANE0 - N_GROUPS), F32), expert_part,
             jnp.zeros((rows, LANES - ROUTER_EXPERT_LANE0 - N_EXPERTS), F32)], axis=1)

    w_router = router_lanes(w_group_router[l], w_expert_router[l])
    b_router = router_lanes(b_group_router[l][None, :], b_expert_router[l][None, :])
    s5p = _s5_params(ssm_lambda_re[l], ssm_lambda_im[l], ssm_log_step[l], ssm_b_re[l], ssm_b_im[l],
                     ssm_c_re[l], ssm_c_im[l], ssm_d[l], S5_CHUNK // SUBLANES)
    bias = sb_logit_bias[l]

    mn = rmsnorm(mem_prompt.reshape(n_mem, d), g_mem_norm[l], BF16)
    (mem_kv,) = matmul(mn, w_memkv, (F32,), tm=n_mem, tn=1024, name="memory_kv")
    mk_p, mv_p = mem_kv[:, :mem_w], mem_kv[:, mem_w:]

    def mixer_and_moe(x, attend, mem_k, mem_v, s5):
        hn = rmsnorm(x, g_mix_norm[l], BF16)
        u, q, k, kb, v, vb, qm = in_projection(hn, w5, ssm_w)
        o_ssm, h_re, h_im = s5(u)
        o_sb = attend(q, k, v, kb, vb)
        o_mem = memory_attention(qm.reshape(mem_k.shape[0], -1, mem_w), mem_k, mem_v).reshape(-1, mem_w)
        merged = gated_merge(hn, (o_ssm, o_sb, o_mem), w_gates, w_branches)
        (x2,) = matmul(merged, w_out_b, (F32,), res=x, tm=512, tn=d, name="out_projection")
        moe_tile = min(MOE_TILE, x2.shape[0])
        hn2, route, counts = moe_router(x2, g_ffn_norm[l], w_router, b_router, moe_tile)
        y = moe_experts_final_norm(hn2, route, counts, x2, w_gate_up, w_down, g_final, moe_tile)
        return y, k, v, h_re, h_im

    zeros = jnp.zeros((1, n_st), F32)
    y_p, k_p, v_p, hr_p, hi_p = mixer_and_moe(
        x_prompt.reshape(tp, d),
        lambda q, k, v, kb, vb: sb_attention_prompt(q, kb, vb, bias),
        mk_p.reshape(1, n_mem, mem_w), mv_p.reshape(1, n_mem, mem_w),
        lambda u: s5_prompt(u, zeros, zeros, s5p, w_glu_b, b_glu[l], S5_CHUNK))

    def s5_s(u):
        u_tb = u.reshape(bs, ts, ssm_w).transpose(1, 0, 2).reshape(ts * bs, ssm_w)
        o_tb, h_re, h_im = s5_sample(u_tb, state_ssm_re[l].reshape(bs, n_st), state_ssm_im[l].reshape(bs, n_st),
                                     s5p, w_glu_b, b_glu[l], bs, ts)
        return o_tb.reshape(ts, bs, ssm_w).transpose(1, 0, 2).reshape(bs * ts, ssm_w), h_re, h_im

    def attend_s(q, k, v, kb, vb):
        o = sb_attention_sample(q.reshape(bs, ts, sb_w), k.reshape(bs, ts, sb_w), v.reshape(bs, ts, sb_w), bias,
                                cache_k.reshape(-1, PAGE_SIZE * n_heads, SB_HEAD_DIM),
                                cache_v.reshape(-1, PAGE_SIZE * n_heads, SB_HEAD_DIM),
                                page_table + l * cache_k.shape[1])
        return o.reshape(bs * ts, sb_w).astype(BF16)

    y_s, k_s, v_s, hr_s, hi_s = mixer_and_moe(
        x_sample.reshape(bs * ts, d), attend_s,
        cache_mem_k[l].reshape(bs, n_mem, mem_w), cache_mem_v[l].reshape(bs, n_mem, mem_w), s5_s)

    return (y_p.reshape(bp, tp, d), y_s.reshape(bs, ts, d),
            k_p.reshape(1, bp, tp, n_heads, SB_HEAD_DIM), v_p.reshape(1, bp, tp, n_heads, SB_HEAD_DIM),
            k_s.reshape(1, bs, ts, n_heads, SB_HEAD_DIM), v_s.reshape(1, bs, ts, n_heads, SB_HEAD_DIM),
            hr_p.reshape(1, bp, n_groups, n_state), hi_p.reshape(1, bp, n_groups, n_state),
            hr_s.reshape(1, bs, n_groups, n_state), hi_s.reshape(1, bs, n_groups, n_state),
            mk_p.reshape(1, bp, n_mem, MEM_HEADS, mem_dh), mv_p.reshape(1, bp, n_mem, MEM_HEADS, mem_dh))
```
